```python
import math
import jax
import jax.numpy as jnp
from jax import lax
import numpy as np

D_MODEL = 2048
BATCH = 8
SEQ = 4096
DEPTH = 4

GRID_W = 64
CTX_LEN = 256
HEAD_DIM = 128
MIX_WIDTH = D_MODEL
A_HEADS = (MIX_WIDTH // 2) // HEAD_DIM
A_DK = HEAD_DIM
A_DV = HEAD_DIM
A_WIDTH = A_HEADS * A_DV
B_WIDTH = MIX_WIDTH - A_WIDTH
CONV_K = 31
HGRN_CHUNK = 64
C_HEADS = (MIX_WIDTH // 2) // HEAD_DIM
DIFF_D = HEAD_DIM // 2
C_WIDTH = C_HEADS * HEAD_DIM
D_HEADS = (MIX_WIDTH - C_WIDTH) // HEAD_DIM
D_KV_HEADS = 2
D_WIDTH = D_HEADS * HEAD_DIM
KV_WIDTH = D_KV_HEADS * HEAD_DIM
Q_BLOCK = 128
ROPE_THETA = 10000.0
D_FF = 5632
N_EXPERTS = 8
TOP_K = 2
D_FF_EXPERT = 5632
N_EVEN = (DEPTH + 1) // 2
N_ODD = DEPTH // 2
AB_COLS = 5 * A_WIDTH + 2 * B_WIDTH
CD_COLS = 3 * C_WIDTH + D_WIDTH + 2 * KV_WIDTH
EPS = 1e-6
F32 = jnp.float32

kernel_name = "hybrid_hgrn2_conformer_diffattn_gqa_moe_dit"


def _rmsnorm(x, g):
    xf = x.astype(F32)
    y = xf * lax.rsqrt(jnp.mean(xf * xf, axis=-1, keepdims=True) + EPS)
    return (y * g.astype(F32)).astype(x.dtype)


def _split_cols(p, sizes):
    return jnp.split(p, [int(s) for s in np.cumsum(sizes)[:-1]], axis=-1)


def _ada(cond, w, b):
    m = jax.nn.silu(cond) @ w + b
    return [t[:, None, :] for t in jnp.split(m, 6, axis=-1)]


def _axial_rope_tables(n_tokens, head_dim):
    t = jnp.arange(n_tokens)
    axis_dim = head_dim // 2
    inv_freq = ROPE_THETA ** (-jnp.arange(0, axis_dim, 2, dtype=F32) / axis_dim)
    ang = jnp.concatenate([(t // GRID_W).astype(F32)[:, None] * inv_freq,
                           (t % GRID_W).astype(F32)[:, None] * inv_freq], axis=-1)
    return jnp.cos(ang), jnp.sin(ang)


def _apply_axial_rope(x, cos, sin):
    L = x.shape[1]
    q4 = x.shape[-1] // 4
    xs = x.reshape(x.shape[:-1] + (2, 2, q4))
    bshape = (1, L) + (1,) * (x.ndim - 3) + (2, q4)
    cs = cos.reshape(bshape).astype(x.dtype)
    sn = sin.reshape(bshape).astype(x.dtype)
    x1, x2 = xs[..., 0, :], xs[..., 1, :]
    out = jnp.stack([x1 * cs - x2 * sn, x2 * cs + x1 * sn], axis=-2)
    return out.reshape(x.shape)


def _sweep_query_blocks(fn, *qs):
    B, L = qs[0].shape[:2]
    nb = L // Q_BLOCK
    blocks = tuple(jnp.moveaxis(q.reshape((B, nb, Q_BLOCK) + q.shape[2:]), 1, 0) for q in qs)
    out = lax.map(lambda blk: fn(*blk), blocks)
    return jnp.moveaxis(out, 0, 1).reshape(B, L, out.shape[-1])


def _hgrn2_gates(z, lb):
    zf = z.astype(F32)
    f = lb + (1.0 - lb) * jax.nn.sigmoid(zf)
    return (1.0 - lb) * jax.nn.sigmoid(-zf), jnp.log(f)


def _hgrn2_scan(q, k, v, log_f, s0):
    B, L, H, _ = q.shape
    n = L // HGRN_CHUNK

    def chunks(t):
        return jnp.moveaxis(t.reshape((B, n, HGRN_CHUNK) + t.shape[2:]), 1, 0)

    tri = jnp.tril(jnp.ones((HGRN_CHUNK, HGRN_CHUNK), dtype=bool))[None, :, :, None, None]

    def step(state, inp):
        qc, kc, vc, gc = inp
        b = jnp.cumsum(gc, axis=1)
        o_inter = jnp.einsum('bthk,bhkv->bthv', qc * jnp.exp(b), state)
        rel = b[:, :, None] - b[:, None, :]
        decay = jnp.where(tri, jnp.exp(jnp.minimum(rel, 0.0)), 0.0)
        scores = jnp.einsum('bthk,bshk,btshk->bhts', qc, kc, decay)
        o_intra = jnp.einsum('bhts,bshv->bthv', scores, vc)
        b_end = b[:, -1]
        k_end = kc * jnp.exp(b_end[:, None] - b)
        state = jnp.exp(b_end)[..., None] * state + jnp.einsum('bshk,bshv->bhkv', k_end, vc)
        return state, o_inter + o_intra

    s_final, out = lax.scan(step, s0, (chunks(q), chunks(k), chunks(v), chunks(log_f)))
    return jnp.moveaxis(out, 0, 1).reshape(B, L, H, v.shape[-1]), s_final


def _hgrn2_bidir(parts, s0_f, s0_b):
    q, k_f, g_f, k_b, g_b, v = parts
    o_f, s_f = _hgrn2_scan(q, k_f, v, g_f, s0_f)
    flip = lambda t: jnp.flip(t, axis=1)
    o_b, s_b = _hgrn2_scan(flip(q), flip(k_b), flip(v), flip(g_b), s0_b)
    return o_f + flip(o_b), s_f, s_b


def _conv_module(glu, dw_w, dw_b, ln_g, ln_b):
    a, g = jnp.split(glu, 2, axis=-1)
    u = a * jax.nn.sigmoid(g)
    u = lax.conv_general_dilated(u, dw_w[:, None, :].astype(u.dtype), window_strides=(1,),
                                 padding=((CONV_K // 2, CONV_K // 2),),
                                 dimension_numbers=('NWC', 'WIO', 'NWC'),
                                 feature_group_count=B_WIDTH) + dw_b.astype(u.dtype)
    uf = u.astype(F32)
    mu = jnp.mean(uf, axis=-1, keepdims=True)
    var = jnp.mean(jnp.square(uf - mu), axis=-1, keepdims=True)
    un = (uf - mu) * lax.rsqrt(var + EPS) * ln_g.astype(F32) + ln_b.astype(F32)
    return jax.nn.silu(un).astype(glu.dtype)


def _mixer_hgrn2_conv(hx, hc, w_in, lb, onorm_g, dw_w, dw_b, ln_g, ln_b, with_ctx_out):
    def project(h):
        B, L = h.shape[:2]
        q, f_fwd, f_bwd, i, og, glu = _split_cols(h @ w_in, [A_WIDTH] * 5 + [2 * B_WIDTH])
        heads = lambda t: t.reshape(B, L, A_HEADS, -1)
        k_f, g_f = _hgrn2_gates(f_fwd, lb[0])
        k_b, g_b = _hgrn2_gates(f_bwd, lb[1])
        parts = (heads(jax.nn.silu(q.astype(F32))), heads(k_f), heads(g_f),
                 heads(k_b), heads(g_b), heads(i.astype(F32)))
        return parts, og, glu

    def finish(o, og):
        B, L = og.shape[:2]
        o = _rmsnorm(o, onorm_g.reshape(A_HEADS, A_DV)).reshape(B, L, A_WIDTH)
        return (o * jax.nn.silu(og.astype(F32))).astype(og.dtype)

    parts_c, og_c, glu_c = project(hc)
    parts_x, og_x, glu_x = project(hx)
    s0 = jnp.zeros((hx.shape[0], A_HEADS, A_DK, A_DV), F32)
    o_c, s_f, s_b = _hgrn2_bidir(parts_c, s0, s0)
    o_x, _, _ = _hgrn2_bidir(parts_x, s_f, s_b)
    conv = lambda glu: _conv_module(glu, dw_w, dw_b, ln_g, ln_b)
    mix_x = jnp.concatenate([finish(o_x, og_x), conv(glu_x)], axis=-1)
    mix_c = jnp.concatenate([finish(o_c, og_c), conv(glu_c)], axis=-1) if with_ctx_out else None
    return mix_x, mix_c


def _diff_attend(q, k, v, lam):
    s = jnp.einsum('bqhjd,bkhjd->bhjqk', q, k).astype(F32) * (DIFF_D ** -0.5)
    p = jax.nn.softmax(s, axis=-1)
    a = p[:, :, 0] - lam * p[:, :, 1]
    return jnp.einsum('bhqk,bkhe->bqhe', a.astype(v.dtype), v)


def _gqa_attend(q, k, v):
    s = jnp.einsum('bqhgd,bkhd->bhgqk', q, k).astype(F32) * (HEAD_DIM ** -0.5)
    p = jax.nn.softmax(s, axis=-1)
    return jnp.einsum('bhgqk,bkhd->bqhgd', p.astype(v.dtype), v)


def _mixer_diff_gqa(hx, hc, w_in, qn_c, kn_c, lam_p, subln_g, qn_d, kn_d, lam_init,
                    rope_c, rope_d, with_ctx_out):
    def project(h, rope):
        B, L = h.shape[:2]
        cq, ck, cv, dq, dk, dv = _split_cols(h @ w_in, [C_WIDTH] * 3 + [D_WIDTH, KV_WIDTH, KV_WIDTH])
        cq = _rmsnorm(cq.reshape(B, L, C_HEADS, 2, DIFF_D), qn_c)
        ck = _rmsnorm(ck.reshape(B, L, C_HEADS, 2, DIFF_D), kn_c)
        cv = cv.reshape(B, L, C_HEADS, HEAD_DIM)
        dq = _rmsnorm(dq.reshape(B, L, D_KV_HEADS, D_HEADS // D_KV_HEADS, HEAD_DIM), qn_d)
        dk = _rmsnorm(dk.reshape(B, L, D_KV_HEADS, HEAD_DIM), kn_d)
        dv = dv.reshape(B, L, D_KV_HEADS, HEAD_DIM)
        if rope is not None:
            (cos_c, sin_c), (cos_d, sin_d) = rope
            cq, ck = _apply_axial_rope(cq, cos_c, sin_c), _apply_axial_rope(ck, cos_c, sin_c)
            dq, dk = _apply_axial_rope(dq, cos_d, sin_d), _apply_axial_rope(dk, cos_d, sin_d)
        return cq, ck, cv, dq, dk, dv

    lp = lam_p.astype(F32)
    lam = jnp.exp(jnp.sum(lp[0] * lp[1])) - jnp.exp(jnp.sum(lp[2] * lp[3])) + lam_init

    def attend(cq, dq, ck, cv, dk, dv):
        B, Q = cq.shape[:2]
        co = _rmsnorm(_diff_attend(cq, ck, cv, lam), subln_g) * (1.0 - lam_init)
        do = _gqa_attend(dq, dk, dv)
        return jnp.concatenate([co.reshape(B, Q, C_WIDTH), do.reshape(B, Q, D_WIDTH)], axis=-1)

    cq_c, ck_c, cv_c, dq_c, dk_c, dv_c = project(hc, None)
    cq_x, ck_x, cv_x, dq_x, dk_x, dv_x = project(hx, (rope_c, rope_d))
    ck_all = jnp.concatenate([ck_c, ck_x], axis=1)
    cv_all = jnp.concatenate([cv_c, cv_x], axis=1)
    dk_all = jnp.concatenate([dk_c, dk_x], axis=1)
    dv_all = jnp.concatenate([dv_c, dv_x], axis=1)
    mix_x = _sweep_query_blocks(lambda a, b: attend(a, b, ck_all, cv_all, dk_all, dv_all), cq_x, dq_x)
    mix_c = attend(cq_c, dq_c, ck_c, cv_c, dk_c, dv_c) if with_ctx_out else None
    return mix_x, mix_c


def _swiglu(h, w1, w3, w2):
    return (jax.nn.silu(h @ w1) * (h @ w3)) @ w2


def _moe(h, router, w1, w3, w2):
    logits = (h @ router).astype(F32)
    top_v, top_i = lax.top_k(logits, TOP_K)
    gates = jax.nn.softmax(top_v, axis=-1)
    dense_gate = jnp.sum(jax.nn.one_hot(top_i, N_EXPERTS, dtype=F32) * gates[..., None], axis=-2)
    y = jnp.zeros_like(h)
    for e in range(N_EXPERTS):
        y = y + dense_gate[..., e:e + 1].astype(h.dtype) * _swiglu(h, w1[e], w3[e], w2[e])
    return y


def setup_inputs(seed: int = 0) -> dict:
    key = jax.random.key(seed)
    ks = iter(jax.random.split(key, 40))
    nrm = lambda shape, scale: scale * jax.random.normal(next(ks), shape, F32)
    D = D_MODEL
    return {
        "x": nrm((BATCH, SEQ, D), 1.0),
        "c": nrm((BATCH, D), 1.0),
        "ctx": nrm((BATCH, CTX_LEN, D), 1.0),
        "c_ctx": nrm((D,), 1.0),
        "ada_w": nrm((DEPTH, D, 6 * D), D ** -0.5),
        "ada_b": nrm((DEPTH, 6 * D), 0.02),
        "norm_mix_g": 1.0 + nrm((DEPTH, D), 0.05),
        "norm_ffn_g": 1.0 + nrm((DEPTH, D), 0.05),
        "w_out": nrm((DEPTH, MIX_WIDTH, D), MIX_WIDTH ** -0.5),
        "ab_w_in": nrm((N_EVEN, D, AB_COLS), D ** -0.5),
        "hgrn_lb_logits": nrm((N_EVEN, 2, A_HEADS * A_DK), 0.1),
        "hgrn_onorm_g": 1.0 + nrm((N_EVEN, A_WIDTH), 0.05),
        "conv_dw_w": nrm((N_EVEN, CONV_K, B_WIDTH), CONV_K ** -0.5),
        "conv_dw_b": nrm((N_EVEN, B_WIDTH), 0.02),
        "conv_ln_g": 1.0 + nrm((N_EVEN, B_WIDTH), 0.05),
        "conv_ln_b": nrm((N_EVEN, B_WIDTH), 0.02),
        "cd_w_in": nrm((N_ODD, D, CD_COLS), D ** -0.5),
        "diff_qnorm_g": 1.0 + nrm((N_ODD, DIFF_D), 0.05),
        "diff_knorm_g": 1.0 + nrm((N_ODD, DIFF_D), 0.05),
        "diff_lambda": nrm((N_ODD, 4, DIFF_D), 0.1),
        "diff_subln_g": 1.0 + nrm((N_ODD, HEAD_DIM), 0.05),
        "gqa_qnorm_g": 1.0 + nrm((N_ODD, HEAD_DIM), 0.05),
        "gqa_knorm_g": 1.0 + nrm((N_ODD, HEAD_DIM), 0.05),
        "ffn_w1": nrm((N_EVEN, D, D_FF), D ** -0.5),
        "ffn_w3": nrm((N_EVEN, D, D_FF), D ** -0.5),
        "ffn_w2": nrm((N_EVEN, D_FF, D), D_FF ** -0.5),
        "moe_router": nrm((N_ODD, D, N_EXPERTS), D ** -0.5),
        "moe_w1": nrm((N_ODD, N_EXPERTS, D, D_FF_EXPERT), D ** -0.5),
        "moe_w3": nrm((N_ODD, N_EXPERTS, D, D_FF_EXPERT), D ** -0.5),
        "moe_w2": nrm((N_ODD, N_EXPERTS, D_FF_EXPERT, D), D_FF_EXPERT ** -0.5),
    }


def reference(x, c, ctx, c_ctx, ada_w, ada_b, norm_mix_g, norm_ffn_g, w_out,
              ab_w_in, hgrn_lb_logits, hgrn_onorm_g, conv_dw_w, conv_dw_b, conv_ln_g, conv_ln_b,
              cd_w_in, diff_qnorm_g, diff_knorm_g, diff_lambda, diff_subln_g, gqa_qnorm_g, gqa_knorm_g,
              ffn_w1, ffn_w3, ffn_w2, moe_router, moe_w1, moe_w3, moe_w2):
    L = x.shape[1]
    rope_c = _axial_rope_tables(L, DIFF_D)
    rope_d = _axial_rope_tables(L, HEAD_DIM)
    lbs = jnp.cumsum(jax.nn.softmax(hgrn_lb_logits.astype(F32), axis=0), axis=0)
    lbs = lbs - lbs[:1]
    c_ctx_row = c_ctx[None]
    for li in range(DEPTH):
        j = li // 2
        need_ctx = li < DEPTH - 1
        sh1, sc1, g1, sh2, sc2, g2 = _ada(c, ada_w[li], ada_b[li])
        csh1, csc1, cg1, csh2, csc2, cg2 = _ada(c_ctx_row, ada_w[li], ada_b[li])
        hx = _rmsnorm(x, norm_mix_g[li]) * (1.0 + sc1) + sh1
        hc = _rmsnorm(ctx, norm_mix_g[li]) * (1.0 + csc1) + csh1
        if li % 2 == 0:
            mx, mc = _mixer_hgrn2_conv(hx, hc, ab_w_in[j], lbs[j], hgrn_onorm_g[j], conv_dw_w[j],
                                       conv_dw_b[j], conv_ln_g[j], conv_ln_b[j], need_ctx)
        else:
            mx, mc = _mixer_diff_gqa(hx, hc, cd_w_in[j], diff_qnorm_g[j], diff_knorm_g[j],
                                     diff_lambda[j], diff_subln_g[j], gqa_qnorm_g[j], gqa_knorm_g[j],
                                     0.8 - 0.6 * math.exp(-0.3 * li), rope_c, rope_d, need_ctx)
        x = x + g1 * (mx @ w_out[li])
        if need_ctx:
            ctx = ctx + cg1 * (mc @ w_out[li])

        def channel_mixer(h):
            if li % 2 == 0:
                return _swiglu(h, ffn_w1[j], ffn_w3[j], ffn_w2[j])
            return _moe(h, moe_router[j], moe_w1[j], moe_w3[j], moe_w2[j])

        x = x + g2 * channel_mixer(_rmsnorm(x, norm_ffn_g[li]) * (1.0 + sc2) + sh2)
        if need_ctx:
            ctx = ctx + cg2 * channel_mixer(_rmsnorm(ctx, norm_ffn_g[li]) * (1.0 + csc2) + csh2)
    return x
```

```python
import functools
import math

import numpy as np
import jax
import jax.numpy as jnp
from jax import lax
from jax.experimental import pallas as pl
from jax.experimental.pallas import tpu as pltpu

F32 = jnp.float32
BF16 = jnp.bfloat16
EPS = 1e-6
HEAD_DIM = 128
GRID_W = 64
ROPE_THETA = 10000.0
TOP_K = 2
NEG_BIG = -1e30
V7X_LANES = 128
V7X_SUBLANES = 8
MIB = 1 << 20


def _cparams(semantics, vmem_mib):
    return pltpu.CompilerParams(dimension_semantics=semantics, vmem_limit_bytes=int(vmem_mib * MIB))


def _pick(n, pref, mult):
    if n <= pref:
        return n
    t = (pref // mult) * mult
    while t >= mult:
        if n % t == 0:
            return t
        t -= mult
    return n


def _dot(a, b):
    return jnp.dot(a, b, preferred_element_type=F32)


def _dot_nt(a, b):
    return lax.dot_general(a, b, (((1,), (1,)), ((), ())), preferred_element_type=F32)


def _dot_tn(a, b):
    return lax.dot_general(a, b, (((0,), (0,)), ((), ())), preferred_element_type=F32)


def _silu(x):
    return x * jax.nn.sigmoid(x)


def _norm_modulate(x, g, sc, sh):
    ms = jnp.mean(x * x, axis=-1, keepdims=True)
    return (x * lax.rsqrt(ms + EPS)) * g * (1.0 + sc) + sh


def _ada_body(c_ref, w_ref, b_ref, o_ref):
    cond = c_ref[...]
    a = _silu(cond).astype(BF16)
    o_ref[0] = _dot(a, w_ref[0].astype(BF16)) + b_ref[0]


def _ada_all(cond, ada_w, ada_b):
    depth, d, n = ada_w.shape
    r = cond.shape[0]
    tn = _pick(n, 1024, V7X_LANES)
    return pl.pallas_call(
        _ada_body,
        grid=(depth, n // tn),
        in_specs=[
            pl.BlockSpec((r, d), lambda l, j: (0, 0)),
            pl.BlockSpec((1, d, tn), lambda l, j: (l, 0, j)),
            pl.BlockSpec((1, 1, tn), lambda l, j: (l, 0, j)),
        ],
        out_specs=pl.BlockSpec((1, r, tn), lambda l, j: (l, 0, j)),
        out_shape=jax.ShapeDtypeStruct((depth, r, n), F32),
        compiler_params=_cparams(("arbitrary", "arbitrary"), 40),
        name="ada_modulation",
    )(cond, ada_w, ada_b.reshape(depth, 1, n))


def _norm_mm_body(x_ref, g_ref, sc_ref, sh_ref, w_ref, o_ref, h_ref):
    @pl.when(pl.program_id(2) == 0)
    def _():
        h_ref[...] = _norm_modulate(x_ref[0], g_ref[...], sc_ref[0], sh_ref[0]).astype(BF16)

    res = _dot(h_ref[...], w_ref[...]).astype(o_ref.dtype)
    for c in range(o_ref.shape[1]):
        o_ref[0, c] = res[:, c * V7X_LANES:(c + 1) * V7X_LANES]


def _norm_mm(x, g, sc, sh, w):
    b, l, d = x.shape
    n = w.shape[1]
    tm = _pick(l, 1024, V7X_SUBLANES)
    tn = _pick(n, 512, V7X_LANES)
    nc = tn // V7X_LANES
    per_batch = sc.shape[0] > 1
    mod_idx = (lambda bi, i, j: (bi, 0, 0)) if per_batch else (lambda bi, i, j: (0, 0, 0))
    return pl.pallas_call(
        _norm_mm_body,
        grid=(b, l // tm, n // tn),
        in_specs=[
            pl.BlockSpec((1, tm, d), lambda bi, i, j: (bi, i, 0)),
            pl.BlockSpec((1, d), lambda bi, i, j: (0, 0)),
            pl.BlockSpec((1, 1, d), mod_idx),
            pl.BlockSpec((1, 1, d), mod_idx),
            pl.BlockSpec((d, tn), lambda bi, i, j: (0, j)),
        ],
        out_specs=pl.BlockSpec((1, nc, tm, V7X_LANES), lambda bi, i, j: (bi, j, i, 0)),
        out_shape=jax.ShapeDtypeStruct((b, n // V7X_LANES, l, V7X_LANES), BF16),
        scratch_shapes=[pltpu.VMEM((tm, d), BF16)],
        compiler_params=_cparams(("parallel", "parallel", "arbitrary"), 48),
        name="norm_modulate_project",
    )(x, g.reshape(1, d), sc, sh, w)


def _mm_resid_body(a0_ref, a1_ref, w0_ref, w1_ref, x_ref, gate_ref, o_ref):
    y = _dot(a0_ref[0], w0_ref[...]) + _dot(a1_ref[0], w1_ref[...])
    o_ref[0] = x_ref[0] + gate_ref[0] * y


def _mm_resid(a0, a1, w, x, gate):
    b, l, k0 = a0.shape
    k1 = a1.shape[2]
    n = w.shape[1]
    tm = _pick(l, 1024, V7X_SUBLANES)
    tn = _pick(n, 512, V7X_LANES)
    per_batch = gate.shape[0] > 1
    gate_idx = (lambda bi, i, j: (bi, 0, j)) if per_batch else (lambda bi, i, j: (0, 0, j))
    return pl.pallas_call(
        _mm_resid_body,
        grid=(b, l // tm, n // tn),
        in_specs=[
            pl.BlockSpec((1, tm, k0), lambda bi, i, j: (bi, i, 0)),
            pl.BlockSpec((1, tm, k1), lambda bi, i, j: (bi, i, 0)),
            pl.BlockSpec((k0, tn), lambda bi, i, j: (0, j)),
            pl.BlockSpec((k1, tn), lambda bi, i, j: (k0 // k1, j)),
            pl.BlockSpec((1, tm, tn), lambda bi, i, j: (bi, i, j)),
            pl.BlockSpec((1, 1, tn), gate_idx),
        ],
        out_specs=pl.BlockSpec((1, tm, tn), lambda bi, i, j: (bi, i, j)),
        out_shape=jax.ShapeDtypeStruct((b, l, n), F32),
        compiler_params=_cparams(("parallel", "parallel", "arbitrary"), 48),
        name="project_gated_residual",
    )(a0, a1, w, w, x, gate)


def _router_logits(h32, h_hi, router):
    h_lo = (h32 - h_hi.astype(F32)).astype(BF16)
    r_hi = router.astype(BF16)
    r_lo = (router - r_hi.astype(F32)).astype(BF16)
    return _dot(h_hi, r_hi) + (_dot(h_lo, r_hi) + _dot(h_hi, r_lo))


def _top2_dense_gate(logits, n_experts):
    lane = lax.broadcasted_iota(jnp.int32, logits.shape, 1)
    lg = jnp.where(lane < n_experts, logits, NEG_BIG)
    m1 = jnp.max(lg, axis=-1, keepdims=True)
    i1 = jnp.min(jnp.where(lg == m1, lane, V7X_LANES), axis=-1, keepdims=True)
    sel1 = lane == i1
    lg2 = jnp.where(sel1, NEG_BIG, lg)
    m2 = jnp.max(lg2, axis=-1, keepdims=True)
    i2 = jnp.min(jnp.where(lg2 == m2, lane, V7X_LANES), axis=-1, keepdims=True)
    sel2 = lane == i2
    e2 = jnp.exp(m2 - m1)
    g1 = 1.0 / (1.0 + e2)
    return jnp.where(sel1, g1, 0.0) + jnp.where(sel2, e2 * g1, 0.0)


def _ffn_body(x_ref, g_ref, sc_ref, sh_ref, gate_ref, w1_ref, w3_ref, w2_ref, *rest, n_experts, routed):
    if routed:
        r_ref, o_ref, h_ref, acc_ref, dg_ref = rest
    else:
        o_ref, h_ref, acc_ref = rest
    e = pl.program_id(2)
    f = pl.program_id(3)

    @pl.when((e == 0) & (f == 0))
    def _():
        h32 = _norm_modulate(x_ref[0], g_ref[...], sc_ref[0], sh_ref[0])
        h = h32.astype(BF16)
        h_ref[...] = h
        acc_ref[...] = jnp.zeros_like(acc_ref)
        if routed:
            dg_ref[...] = _top2_dense_gate(_router_logits(h32, h, r_ref[...]), n_experts)

    h = h_ref[...]
    act = _silu(_dot(h, w1_ref[0])) * _dot(h, w3_ref[0])
    if routed:
        dg = dg_ref[...]
        lane = lax.broadcasted_iota(jnp.int32, dg.shape, 1)
        act = act * jnp.sum(jnp.where(lane == e, dg, 0.0), axis=-1, keepdims=True)
    acc_ref[...] += _dot(act.astype(BF16), w2_ref[0])

    @pl.when((e == pl.num_programs(2) - 1) & (f == pl.num_programs(3) - 1))
    def _():
        o_ref[0] = x_ref[0] + gate_ref[0] * acc_ref[...]


def _ffn(x, g, sc, sh, gate, w1, w3, w2, router=None):
    b, l, d = x.shape
    n_e, _, ff = w1.shape
    routed = router is not None
    tm = _pick(l, 512, V7X_SUBLANES)
    tf = _pick(ff, 512, V7X_LANES)
    per_batch = sc.shape[0] > 1
    mod_idx = (lambda bi, i, e, f: (bi, 0, 0)) if per_batch else (lambda bi, i, e, f: (0, 0, 0))
    in_specs = [
        pl.BlockSpec((1, tm, d), lambda bi, i, e, f: (bi, i, 0)),
        pl.BlockSpec((1, d), lambda bi, i, e, f: (0, 0)),
        pl.BlockSpec((1, 1, d), mod_idx),
        pl.BlockSpec((1, 1, d), mod_idx),
        pl.BlockSpec((1, 1, d), mod_idx),
        pl.BlockSpec((1, d, tf), lambda bi, i, e, f: (e, 0, f)),
        pl.BlockSpec((1, d, tf), lambda bi, i, e, f: (e, 0, f)),
        pl.BlockSpec((1, tf, d), lambda bi, i, e, f: (e, f, 0)),
    ]
    args = [x, g.reshape(1, d), sc, sh, gate, w1, w3, w2]
    scratch = [pltpu.VMEM((tm, d), BF16), pltpu.VMEM((tm, d), F32)]
    if routed:
        in_specs.append(pl.BlockSpec((d, V7X_LANES), lambda bi, i, e, f: (0, 0)))
        args.append(router)
        scratch.append(pltpu.VMEM((tm, V7X_LANES), F32))
    return pl.pallas_call(
        functools.partial(_ffn_body, n_experts=n_e, routed=routed),
        grid=(b, l // tm, n_e, ff // tf),
        in_specs=in_specs,
        out_specs=pl.BlockSpec((1, tm, d), lambda bi, i, e, f: (bi, i, 0)),
        out_shape=jax.ShapeDtypeStruct((b, l, d), F32),
        scratch_shapes=scratch,
        compiler_params=_cparams(("parallel", "parallel", "arbitrary", "arbitrary"), 52),
        name="routed_experts" if routed else "dense_swiglu",
    )(*args)


HGRN_BLOCK = 128
HGRN_LEVEL_HALF_WIDTHS = (64, 32, 16, 8)
HGRN_DIAG = 8


@functools.lru_cache(maxsize=None)
def _hgrn_consts(reverse):
    n = HGRN_BLOCK
    t = np.arange(n)[:, None]
    s = np.arange(n)[None, :]
    before = (s >= t) if reverse else (s <= t)
    masks = []
    for w in HGRN_LEVEL_HALF_WIDTHS:
        same = (t // (2 * w)) == (s // (2 * w))
        t_late = (t % (2 * w)) >= w
        s_late = (s % (2 * w)) >= w
        masks.append(same & ~t_late & s_late if reverse else same & t_late & ~s_late)
    masks.append(((t // HGRN_DIAG) == (s // HGRN_DIAG)) & before)
    row_m = np.arange(HGRN_DIAG * n)[:, None] // n
    wsel = (np.arange(n)[None, :] % HGRN_DIAG) == row_m
    return (before.astype(np.float32), np.stack(masks).astype(np.float32), wsel.astype(np.float32))


def _group_row(x, n, r):
    x3 = x.reshape(HGRN_BLOCK // n, n, x.shape[-1])
    return jnp.broadcast_to(x3[:, r:r + 1, :], x3.shape).reshape(x.shape)


def _hgrn_head(zq, zf, v, lb, st, lc, mask_ref, wsel, reverse):
    zf = zf.astype(F32)
    g = jnp.log(lb + (1.0 - lb) * jax.nn.sigmoid(zf))
    kk = (1.0 - lb) * jax.nn.sigmoid(-zf)
    q = _silu(zq.astype(F32))
    g_hi = g.astype(BF16)
    rem = g - g_hi.astype(F32)
    g_mid = rem.astype(BF16)
    g_lo = (rem - g_mid.astype(F32)).astype(BF16)
    b3 = _dot(lc, jnp.concatenate([g_hi, g_mid, g_lo], axis=1))
    n = HGRN_BLOCK
    b = b3[:, :n] + (b3[:, n:2 * n] + b3[:, 2 * n:])
    a = None
    for li, w in enumerate(HGRN_LEVEL_HALF_WIDTHS):
        e = jnp.exp(-jnp.abs(b - _group_row(b, 2 * w, w if reverse else w - 1)))
        al = _dot_nt((q * e).astype(BF16), (kk * e).astype(BF16)) * mask_ref[li]
        a = al if a is None else a + al
    vals = []
    for m in range(HGRN_DIAG):
        e = jnp.exp(jnp.minimum(b - _group_row(b, HGRN_DIAG, m), 0.0))
        vals.append((q * _group_row(kk, HGRN_DIAG, m) * e).astype(BF16))
    a = a + _dot(jnp.concatenate(vals, axis=1), wsel) * mask_ref[len(HGRN_LEVEL_HALF_WIDTHS)]
    end = 0 if reverse else n - 1
    b_end = b[end:end + 1, :]
    o = _dot(a.astype(BF16), v) + _dot_nt((q * jnp.exp(b)).astype(BF16), st.astype(BF16))
    k_end = (kk * jnp.exp(b_end - b)).astype(BF16)
    return o, st * jnp.exp(b_end) + _dot_tn(v, k_end)


def _hgrn_body(q_ref, f_ref, v_ref, lb_ref, s0_ref, lc_ref, mask_ref, wsel_ref, o_ref, st_ref, *, reverse, n_heads):
    @pl.when(pl.program_id(1) == 0)
    def _():
        st_ref[...] = s0_ref[...]

    def head(h, carry):
        o, st_new = _hgrn_head(q_ref[0, h], f_ref[0, h], v_ref[0, h], lb_ref[h], st_ref[0, h],
                               lc_ref[...], mask_ref, wsel_ref[...], reverse)
        o_ref[0, h] = o.astype(o_ref.dtype)
        st_ref[0, h] = st_new
        return carry

    lax.fori_loop(0, n_heads, head, 0)


def _hgrn_scan(proj, lb, s0, n_heads, reverse):
    bsz, _, l, _ = proj.shape
    nb = l // HGRN_BLOCK
    lc, masks, wsel = _hgrn_consts(reverse)
    blk = (lambda i: nb - 1 - i) if reverse else (lambda i: i)
    f_seg = 2 if reverse else 1
    tile = (1, n_heads, HGRN_BLOCK, HEAD_DIM)
    full = lambda shape: pl.BlockSpec(shape, lambda b, i: (0,) * len(shape))
    return pl.pallas_call(
        functools.partial(_hgrn_body, reverse=reverse, n_heads=n_heads),
        grid=(bsz, nb),
        in_specs=[
            pl.BlockSpec(tile, lambda b, i: (b, 0, blk(i), 0)),
            pl.BlockSpec(tile, lambda b, i: (b, f_seg, blk(i), 0)),
            pl.BlockSpec(tile, lambda b, i: (b, 3, blk(i), 0)),
            full((n_heads, 1, HEAD_DIM)),
            pl.BlockSpec((1, n_heads, HEAD_DIM, HEAD_DIM), lambda b, i: (b, 0, 0, 0)),
            full((HGRN_BLOCK, HGRN_BLOCK)),
            full((len(HGRN_LEVEL_HALF_WIDTHS) + 1, HGRN_BLOCK, HGRN_BLOCK)),
            full((HGRN_DIAG * HGRN_BLOCK, HGRN_BLOCK)),
        ],
        out_specs=[
            pl.BlockSpec(tile, lambda b, i: (b, 0, blk(i), 0)),
            pl.BlockSpec((1, n_heads, HEAD_DIM, HEAD_DIM), lambda b, i: (b, 0, 0, 0)),
        ],
        out_shape=[
            jax.ShapeDtypeStruct((bsz, n_heads, l, HEAD_DIM), BF16),
            jax.ShapeDtypeStruct((bsz, n_heads, HEAD_DIM, HEAD_DIM), F32),
        ],
        compiler_params=_cparams(("parallel", "arbitrary"), 32),
        name="hgrn2_scan_bwd" if reverse else "hgrn2_scan_fwd",
    )(proj, proj, proj, lb, s0, jnp.asarray(lc, BF16), jnp.asarray(masks), jnp.asarray(wsel, BF16))


CONV_HALO = 16


def _conv_finish_body(of_ref, ob_ref, og_ref, a_ref, g_ref, ap_ref, gp_ref, an_ref, gn_ref, onorm_ref, w_ref,
                      cb_ref, lng_ref, lnb_ref, oa_ref, oc_ref, ext_ref, y_ref, *, n_heads, n_cc, k_taps):
    i = pl.program_id(1)
    last = pl.num_programs(1) - 1
    tt = a_ref.shape[2]
    lanes = V7X_LANES
    for h in range(n_heads):
        o = of_ref[0, h].astype(F32) + ob_ref[0, h].astype(F32)
        y = o * lax.rsqrt(jnp.mean(o * o, axis=-1, keepdims=True) + EPS) * onorm_ref[h]
        oa_ref[0, :, h * lanes:(h + 1) * lanes] = (y * _silu(og_ref[0, h].astype(F32))).astype(BF16)

    glu = lambda a, g: a.astype(F32) * jax.nn.sigmoid(g.astype(F32))
    first_tap = CONV_HALO - k_taps // 2
    tot = jnp.zeros((tt, lanes), F32)
    for c in range(n_cc):
        ext_ref[c, 0:CONV_HALO] = jnp.where(i > 0, glu(ap_ref[0, c], gp_ref[0, c]), 0.0)
        ext_ref[c, CONV_HALO:CONV_HALO + tt] = glu(a_ref[0, c], g_ref[0, c])
        ext_ref[c, CONV_HALO + tt:2 * CONV_HALO + tt] = jnp.where(i < last, glu(an_ref[0, c], gn_ref[0, c]), 0.0)
        acc = jnp.zeros((tt, lanes), F32)
        for j in range(k_taps):
            acc = acc + w_ref[c, j:j + 1, :] * ext_ref[c, first_tap + j:first_tap + j + tt, :]
        y = acc + cb_ref[c]
        y_ref[c] = y
        tot = tot + y
    inv_n = 1.0 / (n_cc * lanes)
    mu = jnp.sum(tot, axis=-1, keepdims=True) * inv_n
    sq = jnp.zeros((tt, lanes), F32)
    for c in range(n_cc):
        d = y_ref[c] - mu
        sq = sq + d * d
    rstd = lax.rsqrt(jnp.sum(sq, axis=-1, keepdims=True) * inv_n + EPS)
    for c in range(n_cc):
        un = (y_ref[c] - mu) * rstd * lng_ref[c] + lnb_ref[c]
        oc_ref[0, :, c * lanes:(c + 1) * lanes] = _silu(un).astype(BF16)


def _conv_finish(proj, o_f, o_b, onorm_g, dw_w, dw_b, ln_g, ln_b):
    bsz, _, l, lanes = proj.shape
    n_heads = o_f.shape[1]
    k_taps, bw = dw_w.shape
    n_cc = bw // lanes
    assert n_cc == n_heads and k_taps // 2 < CONV_HALO
    tt = _pick(l, 256, CONV_HALO)
    hb = tt // CONV_HALO
    n_halo = l // CONV_HALO
    seg_a = 5 * n_heads // n_cc
    tile = lambda seg: pl.BlockSpec((1, n_heads, tt, lanes), lambda b, i: (b, seg, i, 0))
    halo_prev = lambda seg: pl.BlockSpec((1, n_cc, CONV_HALO, lanes),
                                         lambda b, i: (b, seg, jnp.maximum(i * hb - 1, 0), 0))
    halo_next = lambda seg: pl.BlockSpec((1, n_cc, CONV_HALO, lanes),
                                         lambda b, i: (b, seg, jnp.minimum((i + 1) * hb, n_halo - 1), 0))
    full = lambda shape: pl.BlockSpec(shape, lambda b, i: (0,) * len(shape))
    per_chunk = lambda p: p.reshape(n_cc, 1, lanes)
    w = jnp.transpose(dw_w.reshape(k_taps, n_cc, lanes), (1, 0, 2))
    return pl.pallas_call(
        functools.partial(_conv_finish_body, n_heads=n_heads, n_cc=n_cc, k_taps=k_taps),
        grid=(bsz, l // tt),
        in_specs=[
            pl.BlockSpec((1, n_heads, tt, lanes), lambda b, i: (b, 0, i, 0)),
            pl.BlockSpec((1, n_heads, tt, lanes), lambda b, i: (b, 0, i, 0)),
            tile(4), tile(seg_a), tile(seg_a + 1),
            halo_prev(seg_a), halo_prev(seg_a + 1), halo_next(seg_a), halo_next(seg_a + 1),
            full((n_heads, 1, lanes)), full((n_cc, k_taps, lanes)),
            full((n_cc, 1, lanes)), full((n_cc, 1, lanes)), full((n_cc, 1, lanes)),
        ],
        out_specs=[
            pl.BlockSpec((1, tt, n_heads * lanes), lambda b, i: (b, i, 0)),
            pl.BlockSpec((1, tt, bw), lambda b, i: (b, i, 0)),
        ],
        out_shape=[
            jax.ShapeDtypeStruct((bsz, l, n_heads * lanes), BF16),
            jax.ShapeDtypeStruct((bsz, l, bw), BF16),
        ],
        scratch_shapes=[pltpu.VMEM((n_cc, tt + 2 * CONV_HALO, lanes), F32), pltpu.VMEM((n_cc, tt, lanes), F32)],
        compiler_params=_cparams(("parallel", "parallel"), 32),
        name="hgrn2_gate_conv_module",
    )(o_f, o_b, proj, proj, proj, proj, proj, proj, proj, onorm_g.reshape(n_heads, 1, lanes), w,
      per_chunk(dw_b), per_chunk(ln_g), per_chunk(ln_b))


def _rope_tables(n_tokens, head_dim, scale, rotate):
    lane = np.arange(V7X_LANES) % head_dim
    q4 = head_dim // 4
    col = (lane // (2 * q4)) * q4 + lane % q4
    sign = np.where(lane % (2 * q4) < q4, -1.0, 1.0).astype(np.float32)
    if not rotate:
        return (jnp.full((n_tokens, V7X_LANES), scale, F32), jnp.zeros((n_tokens, V7X_LANES), F32))
    t = jnp.arange(n_tokens)
    axis_dim = head_dim // 2
    inv_freq = ROPE_THETA ** (-jnp.arange(0, axis_dim, 2, dtype=F32) / axis_dim)
    ang = jnp.concatenate([(t // GRID_W).astype(F32)[:, None] * inv_freq,
                           (t % GRID_W).astype(F32)[:, None] * inv_freq], axis=-1)
    return (jnp.cos(ang)[:, col] * scale, jnp.sin(ang)[:, col] * (sign * scale))


def _qk_prep_one(x, gain, gmat, cos, sin, half_w):
    x = x.astype(F32)
    x2 = x * x
    hi = x2.astype(BF16)
    lo = (x2 - hi.astype(F32)).astype(BF16)
    y = x * lax.rsqrt(_dot(hi, gmat) + _dot(lo, gmat) + EPS) * gain
    lane = lax.broadcasted_iota(jnp.int32, y.shape, 1)
    partner = jnp.where(lane % (2 * half_w) < half_w,
                        pltpu.roll(y, V7X_LANES - half_w, 1), pltpu.roll(y, half_w, 1))
    return (y * cos + partner * sin).astype(BF16)


def _qk_prep_body(cq_ref, ck_ref, dq_ref, dk_ref, tab_ref, gain_ref, gmat_ref, o_ref, *, n_heads, n_kv):
    g64, g128 = gmat_ref[0], gmat_ref[1]
    d_half, g_half = HEAD_DIM // 8, HEAD_DIM // 4
    for h in range(n_heads):
        o_ref[0, h] = _qk_prep_one(cq_ref[0, h], gain_ref[0], g64, tab_ref[0], tab_ref[1], d_half)
        o_ref[0, n_heads + h] = _qk_prep_one(ck_ref[0, h], gain_ref[1], g64, tab_ref[2], tab_ref[3], d_half)
        o_ref[0, 2 * n_heads + h] = _qk_prep_one(dq_ref[0, h], gain_ref[2], g128, tab_ref[4], tab_ref[5], g_half)
    for h in range(n_kv):
        o_ref[0, 3 * n_heads + h] = _qk_prep_one(dk_ref[0, h], gain_ref[3], g128, tab_ref[6], tab_ref[7], g_half)


def _qk_prep(proj, gains, n_heads, n_kv, rotate):
    bsz, _, l, lanes = proj.shape
    diff_d = HEAD_DIM // 2
    tabs = jnp.stack(_rope_tables(l, diff_d, diff_d ** -0.5, rotate) + _rope_tables(l, diff_d, 1.0, rotate)
                     + _rope_tables(l, HEAD_DIM, HEAD_DIM ** -0.5, rotate) + _rope_tables(l, HEAD_DIM, 1.0, rotate))
    lane = np.arange(lanes)
    gmat = np.stack([(lane[:, None] // diff_d == lane[None, :] // diff_d) / diff_d, np.full((lanes, lanes), 1.0 / lanes)])
    tt = _pick(l, 256, V7X_SUBLANES)
    tile = lambda n, idx: pl.BlockSpec((1, n, tt, lanes), lambda b, i: (b, idx, i, 0))
    n_out = 3 * n_heads + n_kv
    return pl.pallas_call(
        functools.partial(_qk_prep_body, n_heads=n_heads, n_kv=n_kv),
        grid=(bsz, l // tt),
        in_specs=[
            tile(n_heads, 0), tile(n_heads, 1), tile(n_heads, 3), tile(n_kv, 4 * n_heads // n_kv),
            pl.BlockSpec((8, tt, lanes), lambda b, i: (0, i, 0)),
            pl.BlockSpec((4, 1, lanes), lambda b, i: (0, 0, 0)),
            pl.BlockSpec((2, lanes, lanes), lambda b, i: (0, 0, 0)),
        ],
        out_specs=pl.BlockSpec((1, n_out, tt, lanes), lambda b, i: (b, 0, i, 0)),
        out_shape=jax.ShapeDtypeStruct((bsz, n_out, l, lanes), BF16),
        compiler_params=_cparams(("parallel", "parallel"), 32),
        name="qk_norm_rope",
    )(proj, proj, proj, proj, tabs, gains, jnp.asarray(gmat, BF16))


def _attn_body(lam_ref, subg_ref, q_ref, *refs, n_stack, diff, lam_init, chunks):
    n_src = len(chunks)
    kv_refs = refs[:2 * n_src]
    o_ref, qs_ref, m_ref, l_ref, acc_ref = refs[2 * n_src:]
    tq = q_ref.shape[2]
    if diff:
        q = q_ref[0, 0]
        lane = lax.broadcasted_iota(jnp.int32, q.shape, 1)
        zero = jnp.zeros_like(q)
        qs_ref[0:tq] = jnp.where(lane < HEAD_DIM // 2, q, zero)
        qs_ref[tq:2 * tq] = jnp.where(lane >= HEAD_DIM // 2, q, zero)
    else:
        for g in range(n_stack):
            qs_ref[g * tq:(g + 1) * tq] = q_ref[0, g]
    m_ref[...] = jnp.full(m_ref.shape, NEG_BIG, F32)
    l_ref[...] = jnp.zeros(l_ref.shape, F32)
    acc_ref[...] = jnp.zeros(acc_ref.shape, F32)

    def visit(k, v):
        s = _dot_nt(qs_ref[...], k)
        m_prev = m_ref[...]
        m_new = jnp.maximum(m_prev, jnp.max(s, axis=-1, keepdims=True))
        alpha = jnp.exp(m_prev - m_new)
        p = jnp.exp(s - m_new)
        l_ref[...] = alpha * l_ref[...] + jnp.sum(p, axis=-1, keepdims=True)
        acc_ref[...] = alpha * acc_ref[...] + _dot(p.astype(BF16), v)
        m_ref[...] = m_new

    for src, (tk, n_chunks) in enumerate(chunks):
        k_ref, v_ref = kv_refs[2 * src], kv_refs[2 * src + 1]
        if n_chunks == 1:
            visit(k_ref[0, 0], v_ref[0, 0])
        else:
            def step(ci, carry, k_ref=k_ref, v_ref=v_ref, tk=tk):
                rows = pl.ds(pl.multiple_of(ci * tk, tk), tk)
                visit(k_ref[0, 0, rows, :], v_ref[0, 0, rows, :])
                return carry
            lax.fori_loop(0, n_chunks, step, 0)

    o = acc_ref[...] / l_ref[...]
    if diff:
        lp = lam_ref[...]
        lam = (jnp.exp(jnp.sum(lp[0:1] * lp[1:2], axis=-1, keepdims=True))
               - jnp.exp(jnp.sum(lp[2:3] * lp[3:4], axis=-1, keepdims=True)) + lam_init)
        d = o[0:tq] - lam * o[tq:2 * tq]
        y = d * lax.rsqrt(jnp.mean(d * d, axis=-1, keepdims=True) + EPS) * subg_ref[...]
        o_ref[0] = (y * (1.0 - lam_init)).astype(BF16)
    else:
        for g in range(n_stack):
            o_ref[0, :, g * HEAD_DIM:(g + 1) * HEAD_DIM] = o[g * tq:(g + 1) * tq].astype(BF16)


def _attention(qk_q, kv_srcs, lam_p, subln_g, lam_init, n_heads, n_kv, diff):
    bsz, _, lq, lanes = qk_q.shape
    group = n_heads // n_kv
    n_stack = 2 if diff else group
    n_units = n_heads if diff else n_kv
    tq = _pick(lq, 1024 // n_stack, V7X_SUBLANES)
    rows = n_stack * tq
    if diff:
        q_spec = pl.BlockSpec((1, 1, tq, lanes), lambda b, u, i: (b, u, i, 0))
        k_chunk = lambda u: n_heads + u
        v_chunk = lambda u: 2 * n_heads + u
        out_w = lanes
    else:
        q_spec = pl.BlockSpec((1, group, tq, lanes), lambda b, u, i: (b, 2 * n_heads // group + u, i, 0))
        k_chunk = lambda u: 3 * n_heads + u
        v_chunk = lambda u: 4 * n_heads + n_kv + u
        out_w = group * lanes
    in_specs = [pl.BlockSpec((4, lanes), lambda b, u, i: (0, 0)), pl.BlockSpec((1, lanes), lambda b, u, i: (0, 0)), q_spec]
    args = [lam_p, subln_g.reshape(1, lanes), qk_q]
    chunks = []
    for qk_k, proj in kv_srcs:
        lk = qk_k.shape[2]
        tk = _pick(lk, 512, V7X_SUBLANES)
        chunks.append((tk, lk // tk))
        in_specs.append(pl.BlockSpec((1, 1, lk, lanes), lambda b, u, i: (b, k_chunk(u), 0, 0)))
        in_specs.append(pl.BlockSpec((1, 1, lk, lanes), lambda b, u, i: (b, v_chunk(u), 0, 0)))
        args += [qk_k, proj]
    return pl.pallas_call(
        functools.partial(_attn_body, n_stack=n_stack, diff=diff, lam_init=lam_init, chunks=tuple(chunks)),
        grid=(bsz, n_units, lq // tq),
        in_specs=in_specs,
        out_specs=pl.BlockSpec((1, tq, out_w), lambda b, u, i: (b, i, u)),
        out_shape=jax.ShapeDtypeStruct((bsz, lq, n_heads * lanes), BF16),
        scratch_shapes=[pltpu.VMEM((rows, lanes), BF16), pltpu.VMEM((rows, 1), F32), pltpu.VMEM((rows, 1), F32),
                        pltpu.VMEM((rows, lanes), F32)],
        compiler_params=_cparams(("parallel", "parallel", "arbitrary"), 40),
        name="diff_attention" if diff else "gqa_attention",
    )(*args)


def kernel(x, c, ctx, c_ctx, ada_w, ada_b, norm_mix_g, norm_ffn_g, w_out, ab_w_in, hgrn_lb_logits, hgrn_onorm_g,
           conv_dw_w, conv_dw_b, conv_ln_g, conv_ln_b, cd_w_in, diff_qnorm_g, diff_knorm_g, diff_lambda, diff_subln_g,
           gqa_qnorm_g, gqa_knorm_g, ffn_w1, ffn_w3, ffn_w2, moe_router, moe_w1, moe_w3, moe_w2):
    bsz, _, d = x.shape
    depth = ada_w.shape[0]
    lanes = V7X_LANES
    bf = lambda t: t.astype(BF16)
    n_heads = (d // 2) // HEAD_DIM
    n_kv = (cd_w_in.shape[2] - 4 * n_heads * HEAD_DIM) // (2 * HEAD_DIM)
    n_experts = moe_router.shape[2]

    rows = -(-(bsz + 1) // V7X_SUBLANES) * V7X_SUBLANES
    cond = jnp.zeros((rows, d), F32).at[:bsz].set(c).at[bsz].set(c_ctx)
    mods = _ada_all(cond, ada_w, ada_b)
    lbs = jnp.cumsum(jax.nn.softmax(hgrn_lb_logits.astype(F32), axis=0), axis=0)
    lbs = lbs - lbs[:1]

    for li in range(depth):
        j = li // 2
        need_ctx = li < depth - 1
        sh1, sc1, g1, sh2, sc2, g2 = [t.reshape(bsz, 1, d) for t in jnp.split(mods[li, :bsz], 6, axis=-1)]
        csh1, csc1, cg1, csh2, csc2, cg2 = [t.reshape(1, 1, d) for t in jnp.split(mods[li, bsz:bsz + 1], 6, axis=-1)]
        wo = bf(w_out[li])
        if li % 2 == 0:
            w_in = bf(ab_w_in[j])
            px = _norm_mm(x, norm_mix_g[li], sc1, sh1, w_in)
            pc = _norm_mm(ctx, norm_mix_g[li], csc1, csh1, w_in)
            lb_f = lbs[j, 0].reshape(n_heads, 1, HEAD_DIM)
            lb_b = lbs[j, 1].reshape(n_heads, 1, HEAD_DIM)
            s0 = jnp.zeros((bsz, n_heads, HEAD_DIM, HEAD_DIM), F32)
            ocf, s_f = _hgrn_scan(pc, lb_f, s0, n_heads, False)
            ocb, s_b = _hgrn_scan(pc, lb_b, s0, n_heads, True)
            oxf, _ = _hgrn_scan(px, lb_f, s_f, n_heads, False)
            oxb, _ = _hgrn_scan(px, lb_b, s_b, n_heads, True)
            conv_args = (hgrn_onorm_g[j], conv_dw_w[j], conv_dw_b[j], conv_ln_g[j], conv_ln_b[j])
            mix_x = _conv_finish(px, oxf, oxb, *conv_args)
            mix_c = _conv_finish(pc, ocf, ocb, *conv_args) if need_ctx else None
        else:
            w_in = bf(cd_w_in[j])
            px = _norm_mm(x, norm_mix_g[li], sc1, sh1, w_in)
            pc = _norm_mm(ctx, norm_mix_g[li], csc1, csh1, w_in)
            tile2 = lambda g: jnp.tile(g, lanes // g.shape[0])
            gains = jnp.stack([tile2(diff_qnorm_g[j]), tile2(diff_knorm_g[j]), gqa_qnorm_g[j], gqa_knorm_g[j]])
            gains = gains.reshape(4, 1, lanes).astype(F32)
            qx = _qk_prep(px, gains, n_heads, n_kv, True)
            qc = _qk_prep(pc, gains, n_heads, n_kv, False)
            lam_init = 0.8 - 0.6 * math.exp(-0.3 * li)
            lam_p = jnp.zeros((4, lanes), F32).at[:, :diff_lambda.shape[2]].set(diff_lambda[j])
            attn = lambda q, srcs, diff: _attention(q, srcs, lam_p, diff_subln_g[j], lam_init, n_heads, n_kv, diff)
            srcs_x = [(qc, pc), (qx, px)]
            mix_x = (attn(qx, srcs_x, True), attn(qx, srcs_x, False))
            mix_c = (attn(qc, [(qc, pc)], True), attn(qc, [(qc, pc)], False)) if need_ctx else None
        x = _mm_resid(mix_x[0], mix_x[1], wo, x, g1)
        if need_ctx:
            ctx = _mm_resid(mix_c[0], mix_c[1], wo, ctx, cg1)

        if li % 2 == 0:
            ffn_w = (bf(ffn_w1[j:j + 1]), bf(ffn_w3[j:j + 1]), bf(ffn_w2[j:j + 1]), None)
        else:
            router = jnp.zeros((d, lanes), F32).at[:, :n_experts].set(moe_router[j])
            ffn_w = (bf(moe_w1[j]), bf(moe_w3[j]), bf(moe_w2[j]), router)
        x = _ffn(x, norm_ffn_g[li], sc2, sh2, g2, *ffn_w)
        if need_ctx:
            ctx = _ffn(ctx, norm_ffn_g[li], csc2, csh2, cg2, *ffn_w)
    return x
```

```python
import functools
import math

import numpy as np
import jax
import jax.numpy as jnp
from jax import lax
from jax.experimental import pallas as pl
from jax.experimental.pallas import tpu as pltpu

F32 = jnp.float32
BF16 = jnp.bfloat16
EPS = 1e-6
HEAD_DIM = 128
GRID_W = 64
ROPE_THETA = 10000.0
TOP_K = 2
NEG_BIG = -1e30
V7X_LANES = 128
V7X_SUBLANES = 8
MIB = 1 << 20


def _cparams(semantics, vmem_mib):
    return pltpu.CompilerParams(dimension_semantics=semantics, vmem_limit_bytes=int(vmem_mib * MIB))


def _pick(n, pref, mult):
    if n <= pref:
        return n
    t = (pref // mult) * mult
    while t >= mult:
        if n % t == 0:
            return t
        t -= mult
    return n


def _dot(a, b):
    return jnp.dot(a, b, preferred_element_type=F32)


def _dot_nt(a, b):
    return lax.dot_general(a, b, (((1,), (1,)), ((), ())), preferred_element_type=F32)


def _dot_tn(a, b):
    return lax.dot_general(a, b, (((0,), (0,)), ((), ())), preferred_element_type=F32)


def _silu(x):
    return x * jax.nn.sigmoid(x)


def _norm_modulate(x, g, sc, sh):
    ms = jnp.mean(x * x, axis=-1, keepdims=True)
    return (x * lax.rsqrt(ms + EPS)) * g * (1.0 + sc) + sh


def _ada_body(c_ref, w_ref, b_ref, o_ref):
    cond = c_ref[...]
    a = _silu(cond).astype(BF16)
    o_ref[0] = _dot(a, w_ref[0].astype(BF16)) + b_ref[0]


def _ada_all(cond, ada_w, ada_b):
    depth, d, n = ada_w.shape
    r = cond.shape[0]
    tn = _pick(n, 1024, V7X_LANES)
    return pl.pallas_call(
        _ada_body,
        grid=(depth, n // tn),
        in_specs=[
            pl.BlockSpec((r, d), lambda l, j: (0, 0)),
            pl.BlockSpec((1, d, tn), lambda l, j: (l, 0, j)),
            pl.BlockSpec((1, 1, tn), lambda l, j: (l, 0, j)),
        ],
        out_specs=pl.BlockSpec((1, r, tn), lambda l, j: (l, 0, j)),
        out_shape=jax.ShapeDtypeStruct((depth, r, n), F32),
        compiler_params=_cparams(("arbitrary", "arbitrary"), 40),
        name="ada_modulation",
    )(cond, ada_w, ada_b.reshape(depth, 1, n))


def _norm_mm_body(x_ref, g_ref, sc_ref, sh_ref, w_ref, o_ref, h_ref):
    @pl.when(pl.program_id(2) == 0)
    def _():
        h_ref[...] = _norm_modulate(x_ref[0], g_ref[...], sc_ref[0], sh_ref[0]).astype(BF16)

    res = _dot(h_ref[...], w_ref[...]).astype(o_ref.dtype)
    for c in range(o_ref.shape[1]):
        o_ref[0, c] = res[:, c * V7X_LANES:(c + 1) * V7X_LANES]


def _norm_mm(x, g, sc, sh, w):
    b, l, d = x.shape
    n = w.shape[1]
    tm = _pick(l, 1024, V7X_SUBLANES)
    tn = _pick(n, 512, V7X_LANES)
    nc = tn // V7X_LANES
    per_batch = sc.shape[0] > 1
    mod_idx = (lambda bi, i, j: (bi, 0, 0)) if per_batch else (lambda bi, i, j: (0, 0, 0))
    return pl.pallas_call(
        _norm_mm_body,
        grid=(b, l // tm, n // tn),
        in_specs=[
            pl.BlockSpec((1, tm, d), lambda bi, i, j: (bi, i, 0)),
            pl.BlockSpec((1, d), lambda bi, i, j: (0, 0)),
            pl.BlockSpec((1, 1, d), mod_idx),
            pl.BlockSpec((1, 1, d), mod_idx),
            pl.BlockSpec((d, tn), lambda bi, i, j: (0, j)),
        ],
        out_specs=pl.BlockSpec((1, nc, tm, V7X_LANES), lambda bi, i, j: (bi, j, i, 0)),
        out_shape=jax.ShapeDtypeStruct((b, n // V7X_LANES, l, V7X_LANES), BF16),
        scratch_shapes=[pltpu.VMEM((tm, d), BF16)],
        compiler_params=_cparams(("parallel", "parallel", "arbitrary"), 48),
        name="norm_modulate_project",
    )(x, g.reshape(1, d), sc, sh, w)


def _mm_resid_body(a0_ref, a1_ref, w0_ref, w1_ref, x_ref, gate_ref, o_ref):
    y = _dot(a0_ref[0], w0_ref[...]) + _dot(a1_ref[0], w1_ref[...])
    o_ref[0] = x_ref[0] + gate_ref[0] * y


def _mm_resid(a0, a1, w, x, gate):
    b, l, k0 = a0.shape
    k1 = a1.shape[2]
    n = w.shape[1]
    tm = _pick(l, 1024, V7X_SUBLANES)
    tn = _pick(n, 512, V7X_LANES)
    per_batch = gate.shape[0] > 1
    gate_idx = (lambda bi, i, j: (bi, 0, j)) if per_batch else (lambda bi, i, j: (0, 0, j))
    return pl.pallas_call(
        _mm_resid_body,
        grid=(b, l // tm, n // tn),
        in_specs=[
            pl.BlockSpec((1, tm, k0), lambda bi, i, j: (bi, i, 0)),
            pl.BlockSpec((1, tm, k1), lambda bi, i, j: (bi, i, 0)),
            pl.BlockSpec((k0, tn), lambda bi, i, j: (0, j)),
            pl.BlockSpec((k1, tn), lambda bi, i, j: (k0 // k1, j)),
            pl.BlockSpec((1, tm, tn), lambda bi, i, j: (bi, i, j)),
            pl.BlockSpec((1, 1, tn), gate_idx),
        ],
        out_specs=pl.BlockSpec((1, tm, tn), lambda bi, i, j: (bi, i, j)),
        out_shape=jax.ShapeDtypeStruct((b, l, n), F32),
        compiler_params=_cparams(("parallel", "parallel", "arbitrary"), 48),
        name="project_gated_residual",
    )(a0, a1, w, w, x, gate)


def _router_logits(h32, h_hi, router):
    h_lo = (h32 - h_hi.astype(F32)).astype(BF16)
    r_hi = router.astype(BF16)
    r_lo = (router - r_hi.astype(F32)).astype(BF16)
    return _dot(h_hi, r_hi) + (_dot(h_lo, r_hi) + _dot(h_hi, r_lo))


def _top2(logits, n_experts):
    lane = lax.broadcasted_iota(jnp.int32, logits.shape, 1)
    lg = jnp.where(lane < n_experts, logits, NEG_BIG)
    m1 = jnp.max(lg, axis=-1, keepdims=True)
    i1 = jnp.min(jnp.where(lg == m1, lane, V7X_LANES), axis=-1, keepdims=True)
    lg2 = jnp.where(lane == i1, NEG_BIG, lg)
    m2 = jnp.max(lg2, axis=-1, keepdims=True)
    i2 = jnp.min(jnp.where(lg2 == m2, lane, V7X_LANES), axis=-1, keepdims=True)
    e2 = jnp.exp(m2 - m1)
    g1 = 1.0 / (1.0 + e2)
    return jnp.where(lane == 0, i1.astype(F32),
                     jnp.where(lane == 1, i2.astype(F32), jnp.where(lane == 2, g1, jnp.where(lane == 3, e2 * g1, 0.0))))


def _ffn_body(x_ref, g_ref, sc_ref, sh_ref, gate_ref, w1_ref, w3_ref, w2_ref, o_ref, h_ref, acc_ref):
    f = pl.program_id(2)

    @pl.when(f == 0)
    def _():
        h_ref[...] = _norm_modulate(x_ref[0], g_ref[...], sc_ref[0], sh_ref[0]).astype(BF16)
        acc_ref[...] = jnp.zeros_like(acc_ref)

    h = h_ref[...]
    act = _silu(_dot(h, w1_ref[...])) * _dot(h, w3_ref[...])
    acc_ref[...] += _dot(act.astype(BF16), w2_ref[...])

    @pl.when(f == pl.num_programs(2) - 1)
    def _():
        o_ref[0] = x_ref[0] + gate_ref[0] * acc_ref[...]


def _ffn(x, g, sc, sh, gate, w1, w3, w2):
    b, l, d = x.shape
    ff = w1.shape[1]
    tm = _pick(l, 512, V7X_SUBLANES)
    tf = _pick(ff, 512, V7X_LANES)
    per_batch = sc.shape[0] > 1
    mod_idx = (lambda bi, i, f: (bi, 0, 0)) if per_batch else (lambda bi, i, f: (0, 0, 0))
    return pl.pallas_call(
        _ffn_body,
        grid=(b, l // tm, ff // tf),
        in_specs=[
            pl.BlockSpec((1, tm, d), lambda bi, i, f: (bi, i, 0)),
            pl.BlockSpec((1, d), lambda bi, i, f: (0, 0)),
            pl.BlockSpec((1, 1, d), mod_idx),
            pl.BlockSpec((1, 1, d), mod_idx),
            pl.BlockSpec((1, 1, d), mod_idx),
            pl.BlockSpec((d, tf), lambda bi, i, f: (0, f)),
            pl.BlockSpec((d, tf), lambda bi, i, f: (0, f)),
            pl.BlockSpec((tf, d), lambda bi, i, f: (f, 0)),
        ],
        out_specs=pl.BlockSpec((1, tm, d), lambda bi, i, f: (bi, i, 0)),
        out_shape=jax.ShapeDtypeStruct((b, l, d), F32),
        scratch_shapes=[pltpu.VMEM((tm, d), BF16), pltpu.VMEM((tm, d), F32)],
        compiler_params=_cparams(("parallel", "parallel", "arbitrary"), 52),
        name="dense_swiglu",
    )(x, g.reshape(1, d), sc, sh, gate, w1, w3, w2)


MOE_TILE = 512
GATHER_ROWS = 256


def _route_body(x_ref, g_ref, sc_ref, sh_ref, r_ref, h_ref, info_ref, *, n_experts):
    h32 = _norm_modulate(x_ref[0], g_ref[...], sc_ref[0], sh_ref[0])
    h_ref[0] = h32
    info_ref[0] = _top2(_router_logits(h32, h32.astype(BF16), r_ref[...]), n_experts)


def _route(x, g, sc, sh, router, n_experts):
    b, l, d = x.shape
    tm = _pick(l, 512, V7X_SUBLANES)
    per_batch = sc.shape[0] > 1
    mod_idx = (lambda bi, i: (bi, 0, 0)) if per_batch else (lambda bi, i: (0, 0, 0))
    return pl.pallas_call(
        functools.partial(_route_body, n_experts=n_experts),
        grid=(b, l // tm),
        in_specs=[
            pl.BlockSpec((1, tm, d), lambda bi, i: (bi, i, 0)),
            pl.BlockSpec((1, d), lambda bi, i: (0, 0)),
            pl.BlockSpec((1, 1, d), mod_idx),
            pl.BlockSpec((1, 1, d), mod_idx),
            pl.BlockSpec((d, V7X_LANES), lambda bi, i: (0, 0)),
        ],
        out_specs=[pl.BlockSpec((1, tm, d), lambda bi, i: (bi, i, 0)),
                   pl.BlockSpec((1, tm, V7X_LANES), lambda bi, i: (bi, i, 0))],
        out_shape=[jax.ShapeDtypeStruct((b, l, d), F32), jax.ShapeDtypeStruct((b, l, V7X_LANES), F32)],
        compiler_params=_cparams(("parallel", "parallel"), 40),
        name="moe_route",
    )(x, g.reshape(1, d), sc, sh, router)


def _route_plan(info, n_experts, tm):
    n_tok = info.shape[0]
    ea = jnp.concatenate([info[:, 0], info[:, 1]]).astype(jnp.int32)
    ga = jnp.concatenate([info[:, 2], info[:, 3]])
    onehot = (ea[:, None] == jnp.arange(n_experts, dtype=jnp.int32)[None, :]).astype(jnp.int32)
    rank = jnp.sum((jnp.cumsum(onehot, axis=0) - onehot) * onehot, axis=1)
    counts = jnp.sum(onehot, axis=0)
    padded = ((counts + tm - 1) // tm) * tm
    ends = jnp.cumsum(padded)
    slot = jnp.sum(onehot * (ends - padded)[None, :], axis=1) + rank
    n_rows = 2 * n_tok + n_experts * tm
    n_tiles = n_rows // tm
    tok = jnp.tile(jnp.arange(n_tok, dtype=jnp.int32), 2)
    row_token = jnp.zeros((n_rows,), jnp.int32).at[slot].set(tok)
    row_gate = jnp.zeros((n_rows,), F32).at[slot].set(ga).reshape(n_rows, 1)
    starts = jnp.arange(n_tiles, dtype=jnp.int32) * tm
    tile_expert = jnp.minimum(jnp.sum((starts[:, None] >= ends[None, :]).astype(jnp.int32), axis=1), n_experts - 1)
    tile_valid = (starts < ends[-1]).astype(jnp.int32)
    return slot, row_token, row_gate, tile_expert, tile_valid


def _row_copy(src_ref, dst_ref, sem, src_row, dst_row):
    return pltpu.make_async_copy(src_ref.at[pl.ds(src_row, 1)], dst_ref.at[pl.ds(dst_row, 1)], sem)


def _gather_body(idx_ref, src_ref, dst_ref, sem, *, rows):
    base = pl.program_id(0) * rows

    def issue(r, carry):
        _row_copy(src_ref, dst_ref, sem, idx_ref[0, 0, r], base + r).start()
        return carry

    lax.fori_loop(0, rows, issue, 0)

    def drain(r, carry):
        _row_copy(src_ref, dst_ref, sem, 0, base + r).wait()
        return carry

    lax.fori_loop(0, rows, drain, 0)


def _row_gather(src, idx):
    m = idx.shape[0]
    d = src.shape[1]
    rows = _pick(m, GATHER_ROWS, V7X_SUBLANES)
    return pl.pallas_call(
        functools.partial(_gather_body, rows=rows),
        grid=(m // rows,),
        in_specs=[
            pl.BlockSpec((1, 1, rows), lambda i: (i, 0, 0), memory_space=pltpu.SMEM),
            pl.BlockSpec(memory_space=pl.ANY),
        ],
        out_specs=pl.BlockSpec(memory_space=pl.ANY),
        out_shape=jax.ShapeDtypeStruct((m, d), src.dtype),
        scratch_shapes=[pltpu.SemaphoreType.DMA(())],
        compiler_params=_cparams(("arbitrary",), 16),
        name="moe_row_gather",
    )(idx.reshape(m // rows, 1, rows), src)


def _experts_body(te_ref, tv_ref, hs_ref, gate_ref, w1_ref, w3_ref, w2_ref, o_ref, hb_ref):
    i = pl.program_id(0)
    f = pl.program_id(1)
    last = pl.num_programs(1) - 1
    valid = tv_ref[i] != 0

    @pl.when(valid & (f == 0))
    def _():
        hb_ref[...] = hs_ref[...].astype(BF16)

    @pl.when(f == 0)
    def _():
        o_ref[...] = jnp.zeros_like(o_ref)

    @pl.when(valid)
    def _():
        h = hb_ref[...]
        act = _silu(_dot(h, w1_ref[0])) * _dot(h, w3_ref[0])
        o_ref[...] += _dot(act.astype(BF16), w2_ref[0])

    @pl.when(valid & (f == last))
    def _():
        o_ref[...] = o_ref[...] * gate_ref[...]


def _experts(hs, row_gate, tile_expert, tile_valid, w1, w3, w2):
    p, d = hs.shape
    ff = w1.shape[2]
    tm = MOE_TILE
    tf = _pick(ff, 512, V7X_LANES)
    nf = ff // tf
    f_eff = lambda i, f, tv: jnp.where(tv[i] != 0, f, nf - 1)
    grid_spec = pltpu.PrefetchScalarGridSpec(
        num_scalar_prefetch=2,
        grid=(p // tm, nf),
        in_specs=[
            pl.BlockSpec((tm, d), lambda i, f, te, tv: (i, 0)),
            pl.BlockSpec((tm, 1), lambda i, f, te, tv: (i, 0)),
            pl.BlockSpec((1, d, tf), lambda i, f, te, tv: (te[i], 0, f_eff(i, f, tv))),
            pl.BlockSpec((1, d, tf), lambda i, f, te, tv: (te[i], 0, f_eff(i, f, tv))),
            pl.BlockSpec((1, tf, d), lambda i, f, te, tv: (te[i], f_eff(i, f, tv), 0)),
        ],
        out_specs=pl.BlockSpec((tm, d), lambda i, f, te, tv: (i, 0)),
        scratch_shapes=[pltpu.VMEM((tm, d), BF16)],
    )
    return pl.pallas_call(
        _experts_body,
        grid_spec=grid_spec,
        out_shape=jax.ShapeDtypeStruct((p, d), F32),
        compiler_params=_cparams(("arbitrary", "arbitrary"), 48),
        name="routed_experts",
    )(tile_expert, tile_valid, hs, row_gate, w1, w3, w2)


def _combine_body(i0_ref, i1_ref, x_ref, gate_ref, ys_ref, o_ref, buf_ref, sem, *, rows):
    def issue(r, carry):
        _row_copy(ys_ref, buf_ref.at[0], sem, i0_ref[0, 0, r], r).start()
        _row_copy(ys_ref, buf_ref.at[1], sem, i1_ref[0, 0, r], r).start()
        return carry

    lax.fori_loop(0, rows, issue, 0)

    def drain(r, carry):
        _row_copy(ys_ref, buf_ref.at[0], sem, 0, r).wait()
        _row_copy(ys_ref, buf_ref.at[1], sem, 0, r).wait()
        return carry

    lax.fori_loop(0, rows, drain, 0)
    o_ref[...] = x_ref[...] + gate_ref[0] * (buf_ref[0] + buf_ref[1])


def _combine(x, gate, ys, slot):
    b, l, d = x.shape
    n_tok = b * l
    rows = _pick(l, GATHER_ROWS, V7X_SUBLANES)
    per_batch = gate.shape[0] > 1
    blocks_per_seq = l // rows
    gate_idx = (lambda i: (i // blocks_per_seq, 0, 0)) if per_batch else (lambda i: (0, 0, 0))
    idx = slot.reshape(2, n_tok // rows, 1, rows)
    out = pl.pallas_call(
        functools.partial(_combine_body, rows=rows),
        grid=(n_tok // rows,),
        in_specs=[
            pl.BlockSpec((1, 1, rows), lambda i: (i, 0, 0), memory_space=pltpu.SMEM),
            pl.BlockSpec((1, 1, rows), lambda i: (i, 0, 0), memory_space=pltpu.SMEM),
            pl.BlockSpec((rows, d), lambda i: (i, 0)),
            pl.BlockSpec((1, 1, d), gate_idx),
            pl.BlockSpec(memory_space=pl.ANY),
        ],
        out_specs=pl.BlockSpec((rows, d), lambda i: (i, 0)),
        out_shape=jax.ShapeDtypeStruct((n_tok, d), F32),
        scratch_shapes=[pltpu.VMEM((2, rows, d), F32), pltpu.SemaphoreType.DMA(())],
        compiler_params=_cparams(("arbitrary",), 32),
        name="moe_combine",
    )(idx[0], idx[1], x.reshape(n_tok, d), gate, ys)
    return out.reshape(b, l, d)


def _moe(x, g, sc, sh, gate, router, w1, w3, w2):
    b, l, d = x.shape
    n_experts = w1.shape[0]
    h, info = _route(x, g, sc, sh, router, n_experts)
    slot, row_token, row_gate, tile_expert, tile_valid = _route_plan(info.reshape(b * l, -1), n_experts, MOE_TILE)
    hs = _row_gather(h.reshape(b * l, d), row_token)
    ys = _experts(hs, row_gate, tile_expert, tile_valid, w1, w3, w2)
    return _combine(x, gate, ys, slot)


HGRN_BLOCK = 128
HGRN_LEVEL_HALF_WIDTHS = (64, 32, 16, 8)
HGRN_DIAG = 8


@functools.lru_cache(maxsize=None)
def _hgrn_consts(reverse):
    n = HGRN_BLOCK
    t = np.arange(n)[:, None]
    s = np.arange(n)[None, :]
    before = (s >= t) if reverse else (s <= t)
    masks = []
    for w in HGRN_LEVEL_HALF_WIDTHS:
        same = (t // (2 * w)) == (s // (2 * w))
        t_late = (t % (2 * w)) >= w
        s_late = (s % (2 * w)) >= w
        masks.append(same & ~t_late & s_late if reverse else same & t_late & ~s_late)
    masks.append(((t // HGRN_DIAG) == (s // HGRN_DIAG)) & before)
    row_m = np.arange(HGRN_DIAG * n)[:, None] // n
    wsel = (np.arange(n)[None, :] % HGRN_DIAG) == row_m
    return (before.astype(np.float32), np.stack(masks).astype(np.float32), wsel.astype(np.float32))


def _group_row(x, n, r):
    x3 = x.reshape(HGRN_BLOCK // n, n, x.shape[-1])
    return jnp.broadcast_to(x3[:, r:r + 1, :], x3.shape).reshape(x.shape)


def _hgrn_head(zq, zf, v, lb, st, lc, mask_ref, wsel, reverse):
    zf = zf.astype(F32)
    g = jnp.log(lb + (1.0 - lb) * jax.nn.sigmoid(zf))
    kk = (1.0 - lb) * jax.nn.sigmoid(-zf)
    q = _silu(zq.astype(F32))
    g_hi = g.astype(BF16)
    rem = g - g_hi.astype(F32)
    g_mid = rem.astype(BF16)
    g_lo = (rem - g_mid.astype(F32)).astype(BF16)
    b3 = _dot(lc, jnp.concatenate([g_hi, g_mid, g_lo], axis=1))
    n = HGRN_BLOCK
    b = b3[:, :n] + (b3[:, n:2 * n] + b3[:, 2 * n:])
    a = None
    for li, w in enumerate(HGRN_LEVEL_HALF_WIDTHS):
        e = jnp.exp(-jnp.abs(b - _group_row(b, 2 * w, w if reverse else w - 1)))
        al = _dot_nt((q * e).astype(BF16), (kk * e).astype(BF16)) * mask_ref[li]
        a = al if a is None else a + al
    vals = []
    for m in range(HGRN_DIAG):
        e = jnp.exp(jnp.minimum(b - _group_row(b, HGRN_DIAG, m), 0.0))
        vals.append((q * _group_row(kk, HGRN_DIAG, m) * e).astype(BF16))
    a = a + _dot(jnp.concatenate(vals, axis=1), wsel) * mask_ref[len(HGRN_LEVEL_HALF_WIDTHS)]
    end = 0 if reverse else n - 1
    b_end = b[end:end + 1, :]
    o = _dot(a.astype(BF16), v) + _dot_nt((q * jnp.exp(b)).astype(BF16), st.astype(BF16))
    k_end = (kk * jnp.exp(b_end - b)).astype(BF16)
    return o, st * jnp.exp(b_end) + _dot_tn(v, k_end)


def _hgrn_body(q_ref, f_ref, v_ref, lb_ref, s0_ref, lc_ref, mask_ref, wsel_ref, o_ref, st_ref, *, reverse, n_heads):
    @pl.when(pl.program_id(1) == 0)
    def _():
        st_ref[...] = s0_ref[...]

    def head(h, carry):
        o, st_new = _hgrn_head(q_ref[0, h], f_ref[0, h], v_ref[0, h], lb_ref[h], st_ref[0, h],
                               lc_ref[...], mask_ref, wsel_ref[...], reverse)
        o_ref[0, h] = o.astype(o_ref.dtype)
        st_ref[0, h] = st_new
        return carry

    lax.fori_loop(0, n_heads, head, 0)


def _hgrn_scan(proj, lb, s0, n_heads, reverse):
    bsz, _, l, _ = proj.shape
    nb = l // HGRN_BLOCK
    lc, masks, wsel = _hgrn_consts(reverse)
    blk = (lambda i: nb - 1 - i) if reverse else (lambda i: i)
    f_seg = 2 if reverse else 1
    tile = (1, n_heads, HGRN_BLOCK, HEAD_DIM)
    full = lambda shape: pl.BlockSpec(shape, lambda b, i: (0,) * len(shape))
    return pl.pallas_call(
        functools.partial(_hgrn_body, reverse=reverse, n_heads=n_heads),
        grid=(bsz, nb),
        in_specs=[
            pl.BlockSpec(tile, lambda b, i: (b, 0, blk(i), 0)),
            pl.BlockSpec(tile, lambda b, i: (b, f_seg, blk(i), 0)),
            pl.BlockSpec(tile, lambda b, i: (b, 3, blk(i), 0)),
            full((n_heads, 1, HEAD_DIM)),
            pl.BlockSpec((1, n_heads, HEAD_DIM, HEAD_DIM), lambda b, i: (b, 0, 0, 0)),
            full((HGRN_BLOCK, HGRN_BLOCK)),
            full((len(HGRN_LEVEL_HALF_WIDTHS) + 1, HGRN_BLOCK, HGRN_BLOCK)),
            full((HGRN_DIAG * HGRN_BLOCK, HGRN_BLOCK)),
        ],
        out_specs=[
            pl.BlockSpec(tile, lambda b, i: (b, 0, blk(i), 0)),
            pl.BlockSpec((1, n_heads, HEAD_DIM, HEAD_DIM), lambda b, i: (b, 0, 0, 0)),
        ],
        out_shape=[
            jax.ShapeDtypeStruct((bsz, n_heads, l, HEAD_DIM), BF16),
            jax.ShapeDtypeStruct((bsz, n_heads, HEAD_DIM, HEAD_DIM), F32),
        ],
        compiler_params=_cparams(("parallel", "arbitrary"), 32),
        name="hgrn2_scan_bwd" if reverse else "hgrn2_scan_fwd",
    )(proj, proj, proj, lb, s0, jnp.asarray(lc, BF16), jnp.asarray(masks), jnp.asarray(wsel, BF16))


CONV_HALO = 16


def _conv_finish_body(of_ref, ob_ref, og_ref, a_ref, g_ref, ap_ref, gp_ref, an_ref, gn_ref, onorm_ref, w_ref,
                      cb_ref, lng_ref, lnb_ref, oa_ref, oc_ref, ext_ref, y_ref, *, n_heads, n_cc, k_taps):
    i = pl.program_id(1)
    last = pl.num_programs(1) - 1
    tt = a_ref.shape[2]
    lanes = V7X_LANES
    for h in range(n_heads):
        o = of_ref[0, h].astype(F32) + ob_ref[0, h].astype(F32)
        y = o * lax.rsqrt(jnp.mean(o * o, axis=-1, keepdims=True) + EPS) * onorm_ref[h]
        oa_ref[0, :, h * lanes:(h + 1) * lanes] = (y * _silu(og_ref[0, h].astype(F32))).astype(BF16)

    glu = lambda a, g: a.astype(F32) * jax.nn.sigmoid(g.astype(F32))
    first_tap = CONV_HALO - k_taps // 2
    tot = jnp.zeros((tt, lanes), F32)
    for c in range(n_cc):
        ext_ref[c, 0:CONV_HALO] = jnp.where(i > 0, glu(ap_ref[0, c], gp_ref[0, c]), 0.0)
        ext_ref[c, CONV_HALO:CONV_HALO + tt] = glu(a_ref[0, c], g_ref[0, c])
        ext_ref[c, CONV_HALO + tt:2 * CONV_HALO + tt] = jnp.where(i < last, glu(an_ref[0, c], gn_ref[0, c]), 0.0)
        acc = jnp.zeros((tt, lanes), F32)
        for j in range(k_taps):
            acc = acc + w_ref[c, j:j + 1, :] * ext_ref[c, first_tap + j:first_tap + j + tt, :]
        y = acc + cb_ref[c]
        y_ref[c] = y
        tot = tot + y
    inv_n = 1.0 / (n_cc * lanes)
    mu = jnp.sum(tot, axis=-1, keepdims=True) * inv_n
    sq = jnp.zeros((tt, lanes), F32)
    for c in range(n_cc):
        d = y_ref[c] - mu
        sq = sq + d * d
    rstd = lax.rsqrt(jnp.sum(sq, axis=-1, keepdims=True) * inv_n + EPS)
    for c in range(n_cc):
        un = (y_ref[c] - mu) * rstd * lng_ref[c] + lnb_ref[c]
        oc_ref[0, :, c * lanes:(c + 1) * lanes] = _silu(un).astype(BF16)


def _conv_finish(proj, o_f, o_b, onorm_g, dw_w, dw_b, ln_g, ln_b):
    bsz, _, l, lanes = proj.shape
    n_heads = o_f.shape[1]
    k_taps, bw = dw_w.shape
    n_cc = bw // lanes
    assert n_cc == n_heads and k_taps // 2 < CONV_HALO
    tt = _pick(l, 256, CONV_HALO)
    hb = tt // CONV_HALO
    n_halo = l // CONV_HALO
    seg_a = 5 * n_heads // n_cc
    tile = lambda seg: pl.BlockSpec((1, n_heads, tt, lanes), lambda b, i: (b, seg, i, 0))
    halo_prev = lambda seg: pl.BlockSpec((1, n_cc, CONV_HALO, lanes),
                                         lambda b, i: (b, seg, jnp.maximum(i * hb - 1, 0), 0))
    halo_next = lambda seg: pl.BlockSpec((1, n_cc, CONV_HALO, lanes),
                                         lambda b, i: (b, seg, jnp.minimum((i + 1) * hb, n_halo - 1), 0))
    full = lambda shape: pl.BlockSpec(shape, lambda b, i: (0,) * len(shape))
    per_chunk = lambda p: p.reshape(n_cc, 1, lanes)
    w = jnp.transpose(dw_w.reshape(k_taps, n_cc, lanes), (1, 0, 2))
    return pl.pallas_call(
        functools.partial(_conv_finish_body, n_heads=n_heads, n_cc=n_cc, k_taps=k_taps),
        grid=(bsz, l // tt),
        in_specs=[
            pl.BlockSpec((1, n_heads, tt, lanes), lambda b, i: (b, 0, i, 0)),
            pl.BlockSpec((1, n_heads, tt, lanes), lambda b, i: (b, 0, i, 0)),
            tile(4), tile(seg_a), tile(seg_a + 1),
            halo_prev(seg_a), halo_prev(seg_a + 1), halo_next(seg_a), halo_next(seg_a + 1),
            full((n_heads, 1, lanes)), full((n_cc, k_taps, lanes)),
            full((n_cc, 1, lanes)), full((n_cc, 1, lanes)), full((n_cc, 1, lanes)),
        ],
        out_specs=[
            pl.BlockSpec((1, tt, n_heads * lanes), lambda b, i: (b, i, 0)),
            pl.BlockSpec((1, tt, bw), lambda b, i: (b, i, 0)),
        ],
        out_shape=[
            jax.ShapeDtypeStruct((bsz, l, n_heads * lanes), BF16),
            jax.ShapeDtypeStruct((bsz, l, bw), BF16),
        ],
        scratch_shapes=[pltpu.VMEM((n_cc, tt + 2 * CONV_HALO, lanes), F32), pltpu.VMEM((n_cc, tt, lanes), F32)],
        compiler_params=_cparams(("parallel", "parallel"), 32),
        name="hgrn2_gate_conv_module",
    )(o_f, o_b, proj, proj, proj, proj, proj, proj, proj, onorm_g.reshape(n_heads, 1, lanes), w,
      per_chunk(dw_b), per_chunk(ln_g), per_chunk(ln_b))


def _rope_tables(n_tokens, head_dim, scale, rotate):
    lane = np.arange(V7X_LANES) % head_dim
    q4 = head_dim // 4
    col = (lane // (2 * q4)) * q4 + lane % q4
    sign = np.where(lane % (2 * q4) < q4, -1.0, 1.0).astype(np.float32)
    if not rotate:
        return (jnp.full((n_tokens, V7X_LANES), scale, F32), jnp.zeros((n_tokens, V7X_LANES), F32))
    t = jnp.arange(n_tokens)
    axis_dim = head_dim // 2
    inv_freq = ROPE_THETA ** (-jnp.arange(0, axis_dim, 2, dtype=F32) / axis_dim)
    ang = jnp.concatenate([(t // GRID_W).astype(F32)[:, None] * inv_freq,
                           (t % GRID_W).astype(F32)[:, None] * inv_freq], axis=-1)
    return (jnp.cos(ang)[:, col] * scale, jnp.sin(ang)[:, col] * (sign * scale))


def _qk_prep_one(x, gain, gmat, cos, sin, half_w):
    x = x.astype(F32)
    x2 = x * x
    hi = x2.astype(BF16)
    lo = (x2 - hi.astype(F32)).astype(BF16)
    y = x * lax.rsqrt(_dot(hi, gmat) + _dot(lo, gmat) + EPS) * gain
    lane = lax.broadcasted_iota(jnp.int32, y.shape, 1)
    partner = jnp.where(lane % (2 * half_w) < half_w,
                        pltpu.roll(y, V7X_LANES - half_w, 1), pltpu.roll(y, half_w, 1))
    return (y * cos + partner * sin).astype(BF16)


def _qk_prep_body(cq_ref, ck_ref, dq_ref, dk_ref, tab_ref, gain_ref, gmat_ref, o_ref, *, n_heads, n_kv):
    g64, g128 = gmat_ref[0], gmat_ref[1]
    d_half, g_half = HEAD_DIM // 8, HEAD_DIM // 4
    for h in range(n_heads):
        o_ref[0, h] = _qk_prep_one(cq_ref[0, h], gain_ref[0], g64, tab_ref[0], tab_ref[1], d_half)
        o_ref[0, n_heads + h] = _qk_prep_one(ck_ref[0, h], gain_ref[1], g64, tab_ref[2], tab_ref[3], d_half)
        o_ref[0, 2 * n_heads + h] = _qk_prep_one(dq_ref[0, h], gain_ref[2], g128, tab_ref[4], tab_ref[5], g_half)
    for h in range(n_kv):
        o_ref[0, 3 * n_heads + h] = _qk_prep_one(dk_ref[0, h], gain_ref[3], g128, tab_ref[6], tab_ref[7], g_half)


def _qk_prep(proj, gains, n_heads, n_kv, rotate):
    bsz, _, l, lanes = proj.shape
    diff_d = HEAD_DIM // 2
    log2e = math.log2(math.e)
    tabs = jnp.stack(_rope_tables(l, diff_d, diff_d ** -0.5 * log2e, rotate) + _rope_tables(l, diff_d, 1.0, rotate)
                     + _rope_tables(l, HEAD_DIM, HEAD_DIM ** -0.5 * log2e, rotate) + _rope_tables(l, HEAD_DIM, 1.0, rotate))
    lane = np.arange(lanes)
    gmat = np.stack([(lane[:, None] // diff_d == lane[None, :] // diff_d) / diff_d, np.full((lanes, lanes), 1.0 / lanes)])
    tt = _pick(l, 256, V7X_SUBLANES)
    tile = lambda n, idx: pl.BlockSpec((1, n, tt, lanes), lambda b, i: (b, idx, i, 0))
    n_out = 3 * n_heads + n_kv
    return pl.pallas_call(
        functools.partial(_qk_prep_body, n_heads=n_heads, n_kv=n_kv),
        grid=(bsz, l // tt),
        in_specs=[
            tile(n_heads, 0), tile(n_heads, 1), tile(n_heads, 3), tile(n_kv, 4 * n_heads // n_kv),
            pl.BlockSpec((8, tt, lanes), lambda b, i: (0, i, 0)),
            pl.BlockSpec((4, 1, lanes), lambda b, i: (0, 0, 0)),
            pl.BlockSpec((2, lanes, lanes), lambda b, i: (0, 0, 0)),
        ],
        out_specs=pl.BlockSpec((1, n_out, tt, lanes), lambda b, i: (b, 0, i, 0)),
        out_shape=jax.ShapeDtypeStruct((bsz, n_out, l, lanes), BF16),
        compiler_params=_cparams(("parallel", "parallel"), 32),
        name="qk_norm_rope",
    )(proj, proj, proj, proj, tabs, gains, jnp.asarray(gmat, BF16))


def _attn_body(lam_ref, subg_ref, q_ref, *refs, n_stack, diff, lam_init, chunks):
    n_src = len(chunks)
    kv_refs = refs[:2 * n_src]
    o_ref, qs_ref, m_ref, acc_ref = refs[2 * n_src:]
    tq = q_ref.shape[2]
    if diff:
        q = q_ref[0, 0]
        lane = lax.broadcasted_iota(jnp.int32, q.shape, 1)
        zero = jnp.zeros_like(q)
        qs_ref[0:tq] = jnp.where(lane < HEAD_DIM // 2, q, zero)
        qs_ref[tq:2 * tq] = jnp.where(lane >= HEAD_DIM // 2, q, zero)
    else:
        for g in range(n_stack):
            qs_ref[g * tq:(g + 1) * tq] = q_ref[0, g]
    m_ref[...] = jnp.full(m_ref.shape, NEG_BIG, F32)
    acc_ref[...] = jnp.zeros(acc_ref.shape, F32)
    lanes = V7X_LANES

    def visit(k, v):
        s = _dot_nt(qs_ref[...], k)
        n_t = s.shape[1] // lanes
        tiles = [s[:, t * lanes:(t + 1) * lanes] for t in range(n_t)]
        mt = tiles[0]
        for t in range(1, n_t):
            mt = jnp.maximum(mt, tiles[t])
        m_prev = m_ref[...]
        m_new = jnp.maximum(m_prev, jnp.max(mt, axis=-1, keepdims=True))
        alpha = jnp.exp2(m_prev - m_new)
        p = jnp.concatenate([jnp.exp2(t - m_new).astype(BF16) for t in tiles], axis=1)
        v_ext = jnp.concatenate([v, jnp.ones(v.shape, BF16)], axis=1)
        acc_ref[...] = jnp.concatenate([alpha, alpha], axis=1) * acc_ref[...] + _dot(p, v_ext)
        m_ref[...] = m_new

    for src, (tk, n_chunks) in enumerate(chunks):
        k_ref, v_ref = kv_refs[2 * src], kv_refs[2 * src + 1]
        if n_chunks == 1:
            visit(k_ref[0, 0], v_ref[0, 0])
        else:
            def step(ci, carry, k_ref=k_ref, v_ref=v_ref, tk=tk):
                rows = pl.ds(pl.multiple_of(ci * tk, tk), tk)
                visit(k_ref[0, 0, rows, :], v_ref[0, 0, rows, :])
                return carry
            lax.fori_loop(0, n_chunks, step, 0, unroll=4 if n_chunks % 4 == 0 else 1)

    o = acc_ref[:, 0:lanes] / acc_ref[:, lanes:2 * lanes]
    if diff:
        lp = lam_ref[...]
        lam = (jnp.exp(jnp.sum(lp[0:1] * lp[1:2], axis=-1, keepdims=True))
               - jnp.exp(jnp.sum(lp[2:3] * lp[3:4], axis=-1, keepdims=True)) + lam_init)
        d = o[0:tq] - lam * o[tq:2 * tq]
        y = d * lax.rsqrt(jnp.mean(d * d, axis=-1, keepdims=True) + EPS) * subg_ref[...]
        o_ref[0] = (y * (1.0 - lam_init)).astype(BF16)
    else:
        for g in range(n_stack):
            o_ref[0, :, g * HEAD_DIM:(g + 1) * HEAD_DIM] = o[g * tq:(g + 1) * tq].astype(BF16)


def _attention(qk_q, kv_srcs, lam_p, subln_g, lam_init, n_heads, n_kv, diff):
    bsz, _, lq, lanes = qk_q.shape
    group = n_heads // n_kv
    n_stack = 2 if diff else group
    n_units = n_heads if diff else n_kv
    tq = _pick(lq, 1024 // n_stack, V7X_SUBLANES)
    rows = n_stack * tq
    if diff:
        q_spec = pl.BlockSpec((1, 1, tq, lanes), lambda b, u, i: (b, u, i, 0))
        k_chunk = lambda u: n_heads + u
        v_chunk = lambda u: 2 * n_heads + u
        out_w = lanes
    else:
        q_spec = pl.BlockSpec((1, group, tq, lanes), lambda b, u, i: (b, 2 * n_heads // group + u, i, 0))
        k_chunk = lambda u: 3 * n_heads + u
        v_chunk = lambda u: 4 * n_heads + n_kv + u
        out_w = group * lanes
    in_specs = [pl.BlockSpec((4, lanes), lambda b, u, i: (0, 0)), pl.BlockSpec((1, lanes), lambda b, u, i: (0, 0)), q_spec]
    args = [lam_p, subln_g.reshape(1, lanes), qk_q]
    chunks = []
    for qk_k, proj in kv_srcs:
        lk = qk_k.shape[2]
        tk = _pick(lk, 512, V7X_SUBLANES)
        chunks.append((tk, lk // tk))
        in_specs.append(pl.BlockSpec((1, 1, lk, lanes), lambda b, u, i: (b, k_chunk(u), 0, 0)))
        in_specs.append(pl.BlockSpec((1, 1, lk, lanes), lambda b, u, i: (b, v_chunk(u), 0, 0)))
        args += [qk_k, proj]
    return pl.pallas_call(
        functools.partial(_attn_body, n_stack=n_stack, diff=diff, lam_init=lam_init, chunks=tuple(chunks)),
        grid=(bsz, n_units, lq // tq),
        in_specs=in_specs,
        out_specs=pl.BlockSpec((1, tq, out_w), lambda b, u, i: (b, i, u)),
        out_shape=jax.ShapeDtypeStruct((bsz, lq, n_heads * lanes), BF16),
        scratch_shapes=[pltpu.VMEM((rows, lanes), BF16), pltpu.VMEM((rows, lanes), F32),
                        pltpu.VMEM((rows, 2 * lanes), F32)],
        compiler_params=_cparams(("parallel", "parallel", "arbitrary"), 40),
        name="diff_attention" if diff else "gqa_attention",
    )(*args)


def kernel(x, c, ctx, c_ctx, ada_w, ada_b, norm_mix_g, norm_ffn_g, w_out, ab_w_in, hgrn_lb_logits, hgrn_onorm_g,
           conv_dw_w, conv_dw_b, conv_ln_g, conv_ln_b, cd_w_in, diff_qnorm_g, diff_knorm_g, diff_lambda, diff_subln_g,
           gqa_qnorm_g, gqa_knorm_g, ffn_w1, ffn_w3, ffn_w2, moe_router, moe_w1, moe_w3, moe_w2):
    bsz, _, d = x.shape
    depth = ada_w.shape[0]
    lanes = V7X_LANES
    bf = lambda t: t.astype(BF16)
    n_heads = (d // 2) // HEAD_DIM
    n_kv = (cd_w_in.shape[2] - 4 * n_heads * HEAD_DIM) // (2 * HEAD_DIM)
    n_experts = moe_router.shape[2]

    rows = -(-(bsz + 1) // V7X_SUBLANES) * V7X_SUBLANES
    cond = jnp.zeros((rows, d), F32).at[:bsz].set(c).at[bsz].set(c_ctx)
    mods = _ada_all(cond, ada_w, ada_b)
    lbs = jnp.cumsum(jax.nn.softmax(hgrn_lb_logits.astype(F32), axis=0), axis=0)
    lbs = lbs - lbs[:1]

    for li in range(depth):
        j = li // 2
        need_ctx = li < depth - 1
        sh1, sc1, g1, sh2, sc2, g2 = [t.reshape(bsz, 1, d) for t in jnp.split(mods[li, :bsz], 6, axis=-1)]
        csh1, csc1, cg1, csh2, csc2, cg2 = [t.reshape(1, 1, d) for t in jnp.split(mods[li, bsz:bsz + 1], 6, axis=-1)]
        wo = bf(w_out[li])
        if li % 2 == 0:
            w_in = bf(ab_w_in[j])
            px = _norm_mm(x, norm_mix_g[li], sc1, sh1, w_in)
            pc = _norm_mm(ctx, norm_mix_g[li], csc1, csh1, w_in)
            lb_f = lbs[j, 0].reshape(n_heads, 1, HEAD_DIM)
            lb_b = lbs[j, 1].reshape(n_heads, 1, HEAD_DIM)
            s0 = jnp.zeros((bsz, n_heads, HEAD_DIM, HEAD_DIM), F32)
            ocf, s_f = _hgrn_scan(pc, lb_f, s0, n_heads, False)
            ocb, s_b = _hgrn_scan(pc, lb_b, s0, n_heads, True)
            oxf, _ = _hgrn_scan(px, lb_f, s_f, n_heads, False)
            oxb, _ = _hgrn_scan(px, lb_b, s_b, n_heads, True)
            conv_args = (hgrn_onorm_g[j], conv_dw_w[j], conv_dw_b[j], conv_ln_g[j], conv_ln_b[j])
            mix_x = _conv_finish(px, oxf, oxb, *conv_args)
            mix_c = _conv_finish(pc, ocf, ocb, *conv_args) if need_ctx else None
        else:
            w_in = bf(cd_w_in[j])
            px = _norm_mm(x, norm_mix_g[li], sc1, sh1, w_in)
            pc = _norm_mm(ctx, norm_mix_g[li], csc1, csh1, w_in)
            tile2 = lambda g: jnp.tile(g, lanes // g.shape[0])
            gains = jnp.stack([tile2(diff_qnorm_g[j]), tile2(diff_knorm_g[j]), gqa_qnorm_g[j], gqa_knorm_g[j]])
            gains = gains.reshape(4, 1, lanes).astype(F32)
            qx = _qk_prep(px, gains, n_heads, n_kv, True)
            qc = _qk_prep(pc, gains, n_heads, n_kv, False)
            lam_init = 0.8 - 0.6 * math.exp(-0.3 * li)
            lam_p = jnp.zeros((4, lanes), F32).at[:, :diff_lambda.shape[2]].set(diff_lambda[j])
            attn = lambda q, srcs, diff: _attention(q, srcs, lam_p, diff_subln_g[j], lam_init, n_heads, n_kv, diff)
            srcs_x = [(qc, pc), (qx, px)]
            mix_x = (attn(qx, srcs_x, True), attn(qx, srcs_x, False))
            mix_c = (attn(qc, [(qc, pc)], True), attn(qc, [(qc, pc)], False)) if need_ctx else None
        x = _mm_resid(mix_x[0], mix_x[1], wo, x, g1)
        if need_ctx:
            ctx = _mm_resid(mix_c[0], mix_c[1], wo, ctx, cg1)

        if li % 2 == 0:
            mixer = functools.partial(_ffn, w1=bf(ffn_w1[j]), w3=bf(ffn_w3[j]), w2=bf(ffn_w2[j]))
        else:
            router = jnp.zeros((d, lanes), F32).at[:, :n_experts].set(moe_router[j])
            mixer = functools.partial(_moe, router=router, w1=bf(moe_w1[j]), w3=bf(moe_w3[j]), w2=bf(moe_w2[j]))
        x = mixer(x, norm_ffn_g[li], sc2, sh2, g2)
        if need_ctx:
            ctx = mixer(ctx, norm_ffn_g[li], csc2, csh2, cg2)
    return x
```

```python
import functools
import math

import numpy as np
import jax
import jax.numpy as jnp
from jax import lax
from jax.experimental import pallas as pl
from jax.experimental.pallas import tpu as pltpu

F32 = jnp.float32
BF16 = jnp.bfloat16
EPS = 1e-6
HEAD_DIM = 128
GRID_W = 64
ROPE_THETA = 10000.0
TOP_K = 2
NEG_BIG = -1e30
V7X_LANES = 128
V7X_SUBLANES = 8
MIB = 1 << 20


def _cparams(semantics, vmem_mib):
    return pltpu.CompilerParams(dimension_semantics=semantics, vmem_limit_bytes=int(vmem_mib * MIB))


def _pick(n, pref, mult):
    if n <= pref:
        return n
    t = (pref // mult) * mult
    while t >= mult:
        if n % t == 0:
            return t
        t -= mult
    return n


def _dot(a, b):
    return jnp.dot(a, b, preferred_element_type=F32)


def _dot_nt(a, b):
    return lax.dot_general(a, b, (((1,), (1,)), ((), ())), preferred_element_type=F32)


def _dot_tn(a, b):
    return lax.dot_general(a, b, (((0,), (0,)), ((), ())), preferred_element_type=F32)


def _silu(x):
    return x * jax.nn.sigmoid(x)


def _norm_modulate(x, g, sc, sh):
    ms = jnp.mean(x * x, axis=-1, keepdims=True)
    return (x * lax.rsqrt(ms + EPS)) * g * (1.0 + sc) + sh


def _ada_body(c_ref, w_ref, b_ref, o_ref):
    cond = c_ref[...]
    a = _silu(cond).astype(BF16)
    o_ref[0] = _dot(a, w_ref[0].astype(BF16)) + b_ref[0]


def _ada_all(cond, ada_w, ada_b):
    depth, d, n = ada_w.shape
    r = cond.shape[0]
    tn = _pick(n, 1024, V7X_LANES)
    return pl.pallas_call(
        _ada_body,
        grid=(depth, n // tn),
        in_specs=[
            pl.BlockSpec((r, d), lambda l, j: (0, 0)),
            pl.BlockSpec((1, d, tn), lambda l, j: (l, 0, j)),
            pl.BlockSpec((1, 1, tn), lambda l, j: (l, 0, j)),
        ],
        out_specs=pl.BlockSpec((1, r, tn), lambda l, j: (l, 0, j)),
        out_shape=jax.ShapeDtypeStruct((depth, r, n), F32),
        compiler_params=_cparams(("arbitrary", "arbitrary"), 40),
        name="ada_modulation",
    )(cond, ada_w, ada_b.reshape(depth, 1, n))


def _norm_mm_body(x_ref, g_ref, sc_ref, sh_ref, w_ref, o_ref, h_ref):
    @pl.when(pl.program_id(2) == 0)
    def _():
        h_ref[...] = _norm_modulate(x_ref[0], g_ref[...], sc_ref[0], sh_ref[0]).astype(BF16)

    res = _dot(h_ref[...], w_ref[...]).astype(o_ref.dtype)
    for c in range(o_ref.shape[1]):
        o_ref[0, c] = res[:, c * V7X_LANES:(c + 1) * V7X_LANES]


def _norm_mm(x, g, sc, sh, w):
    b, l, d = x.shape
    n = w.shape[1]
    tm = _pick(l, 1024, V7X_SUBLANES)
    tn = _pick(n, 1024, 2 * V7X_LANES)
    nc = tn // V7X_LANES
    per_batch = sc.shape[0] > 1
    mod_idx = (lambda bi, i, j: (bi, 0, 0)) if per_batch else (lambda bi, i, j: (0, 0, 0))
    return pl.pallas_call(
        _norm_mm_body,
        grid=(b, l // tm, n // tn),
        in_specs=[
            pl.BlockSpec((1, tm, d), lambda bi, i, j: (bi, i, 0)),
            pl.BlockSpec((1, d), lambda bi, i, j: (0, 0)),
            pl.BlockSpec((1, 1, d), mod_idx),
            pl.BlockSpec((1, 1, d), mod_idx),
            pl.BlockSpec((d, tn), lambda bi, i, j: (0, j)),
        ],
        out_specs=pl.BlockSpec((1, nc, tm, V7X_LANES), lambda bi, i, j: (bi, j, i, 0)),
        out_shape=jax.ShapeDtypeStruct((b, n // V7X_LANES, l, V7X_LANES), BF16),
        scratch_shapes=[pltpu.VMEM((tm, d), BF16)],
        compiler_params=_cparams(("parallel", "parallel", "arbitrary"), 48),
        name="norm_modulate_project",
    )(x, g.reshape(1, d), sc, sh, w)


def _mm_resid_body(a0_ref, a1_ref, w0_ref, w1_ref, x_ref, gate_ref, o_ref):
    y = _dot(a0_ref[0], w0_ref[...]) + _dot(a1_ref[0], w1_ref[...])
    o_ref[0] = x_ref[0] + gate_ref[0] * y


def _mm_resid(a0, a1, w, x, gate):
    b, l, k0 = a0.shape
    k1 = a1.shape[2]
    n = w.shape[1]
    tm = _pick(l, 1024, V7X_SUBLANES)
    tn = _pick(n, 1024, 2 * V7X_LANES)
    per_batch = gate.shape[0] > 1
    gate_idx = (lambda bi, i, j: (bi, 0, j)) if per_batch else (lambda bi, i, j: (0, 0, j))
    return pl.pallas_call(
        _mm_resid_body,
        grid=(b, l // tm, n // tn),
        in_specs=[
            pl.BlockSpec((1, tm, k0), lambda bi, i, j: (bi, i, 0)),
            pl.BlockSpec((1, tm, k1), lambda bi, i, j: (bi, i, 0)),
            pl.BlockSpec((k0, tn), lambda bi, i, j: (0, j)),
            pl.BlockSpec((k1, tn), lambda bi, i, j: (k0 // k1, j)),
            pl.BlockSpec((1, tm, tn), lambda bi, i, j: (bi, i, j)),
            pl.BlockSpec((1, 1, tn), gate_idx),
        ],
        out_specs=pl.BlockSpec((1, tm, tn), lambda bi, i, j: (bi, i, j)),
        out_shape=jax.ShapeDtypeStruct((b, l, n), F32),
        compiler_params=_cparams(("parallel", "parallel", "arbitrary"), 48),
        name="project_gated_residual",
    )(a0, a1, w, w, x, gate)


def _router_logits(h32, h_hi, router):
    h_lo = (h32 - h_hi.astype(F32)).astype(BF16)
    r_hi = router.astype(BF16)
    r_lo = (router - r_hi.astype(F32)).astype(BF16)
    return _dot(h_hi, r_hi) + (_dot(h_lo, r_hi) + _dot(h_hi, r_lo))


def _top2(logits, n_experts):
    lane = lax.broadcasted_iota(jnp.int32, logits.shape, 1)
    lg = jnp.where(lane < n_experts, logits, NEG_BIG)
    m1 = jnp.max(lg, axis=-1, keepdims=True)
    i1 = jnp.min(jnp.where(lg == m1, lane, V7X_LANES), axis=-1, keepdims=True)
    lg2 = jnp.where(lane == i1, NEG_BIG, lg)
    m2 = jnp.max(lg2, axis=-1, keepdims=True)
    i2 = jnp.min(jnp.where(lg2 == m2, lane, V7X_LANES), axis=-1, keepdims=True)
    e2 = jnp.exp(m2 - m1)
    g1 = 1.0 / (1.0 + e2)
    return jnp.where(lane == 0, i1.astype(F32),
                     jnp.where(lane == 1, i2.astype(F32), jnp.where(lane == 2, g1, jnp.where(lane == 3, e2 * g1, 0.0))))


def _ffn_body(x_ref, g_ref, sc_ref, sh_ref, gate_ref, w1_ref, w3_ref, w2_ref, o_ref, h_ref, acc_ref):
    f = pl.program_id(2)

    @pl.when(f == 0)
    def _():
        h_ref[...] = _norm_modulate(x_ref[0], g_ref[...], sc_ref[0], sh_ref[0]).astype(BF16)
        acc_ref[...] = jnp.zeros_like(acc_ref)

    h = h_ref[...]
    act = _silu(_dot(h, w1_ref[...])) * _dot(h, w3_ref[...])
    acc_ref[...] += _dot(act.astype(BF16), w2_ref[...])

    @pl.when(f == pl.num_programs(2) - 1)
    def _():
        o_ref[0] = x_ref[0] + gate_ref[0] * acc_ref[...]


def _ffn(x, g, sc, sh, gate, w1, w3, w2):
    b, l, d = x.shape
    ff = w1.shape[1]
    tm = _pick(l, 512, V7X_SUBLANES)
    tf = _pick(ff, 512, V7X_LANES)
    per_batch = sc.shape[0] > 1
    mod_idx = (lambda bi, i, f: (bi, 0, 0)) if per_batch else (lambda bi, i, f: (0, 0, 0))
    return pl.pallas_call(
        _ffn_body,
        grid=(b, l // tm, ff // tf),
        in_specs=[
            pl.BlockSpec((1, tm, d), lambda bi, i, f: (bi, i, 0)),
            pl.BlockSpec((1, d), lambda bi, i, f: (0, 0)),
            pl.BlockSpec((1, 1, d), mod_idx),
            pl.BlockSpec((1, 1, d), mod_idx),
            pl.BlockSpec((1, 1, d), mod_idx),
            pl.BlockSpec((d, tf), lambda bi, i, f: (0, f)),
            pl.BlockSpec((d, tf), lambda bi, i, f: (0, f)),
            pl.BlockSpec((tf, d), lambda bi, i, f: (f, 0)),
        ],
        out_specs=pl.BlockSpec((1, tm, d), lambda bi, i, f: (bi, i, 0)),
        out_shape=jax.ShapeDtypeStruct((b, l, d), F32),
        scratch_shapes=[pltpu.VMEM((tm, d), BF16), pltpu.VMEM((tm, d), F32)],
        compiler_params=_cparams(("parallel", "parallel", "arbitrary"), 52),
        name="dense_swiglu",
    )(x, g.reshape(1, d), sc, sh, gate, w1, w3, w2)


MOE_TILE = 512
GATHER_ROWS = 256


def _route_body(x_ref, g_ref, sc_ref, sh_ref, r_ref, h_ref, info_ref, *, n_experts):
    h32 = _norm_modulate(x_ref[0], g_ref[...], sc_ref[0], sh_ref[0])
    h_ref[0] = h32
    info_ref[0] = _top2(_router_logits(h32, h32.astype(BF16), r_ref[...]), n_experts)


def _route(x, g, sc, sh, router, n_experts):
    b, l, d = x.shape
    tm = _pick(l, 512, V7X_SUBLANES)
    per_batch = sc.shape[0] > 1
    mod_idx = (lambda bi, i: (bi, 0, 0)) if per_batch else (lambda bi, i: (0, 0, 0))
    return pl.pallas_call(
        functools.partial(_route_body, n_experts=n_experts),
        grid=(b, l // tm),
        in_specs=[
            pl.BlockSpec((1, tm, d), lambda bi, i: (bi, i, 0)),
            pl.BlockSpec((1, d), lambda bi, i: (0, 0)),
            pl.BlockSpec((1, 1, d), mod_idx),
            pl.BlockSpec((1, 1, d), mod_idx),
            pl.BlockSpec((d, V7X_LANES), lambda bi, i: (0, 0)),
        ],
        out_specs=[pl.BlockSpec((1, tm, d), lambda bi, i: (bi, i, 0)),
                   pl.BlockSpec((1, tm, V7X_LANES), lambda bi, i: (bi, i, 0))],
        out_shape=[jax.ShapeDtypeStruct((b, l, d), F32), jax.ShapeDtypeStruct((b, l, V7X_LANES), F32)],
        compiler_params=_cparams(("parallel", "parallel"), 40),
        name="moe_route",
    )(x, g.reshape(1, d), sc, sh, router)


def _route_plan(info, n_experts, tm):
    n_tok = info.shape[0]
    ea = jnp.concatenate([info[:, 0], info[:, 1]]).astype(jnp.int32)
    onehot = (ea[:, None] == jnp.arange(n_experts, dtype=jnp.int32)[None, :]).astype(jnp.int32)
    rank = jnp.sum((jnp.cumsum(onehot, axis=0) - onehot) * onehot, axis=1)
    counts = jnp.sum(onehot, axis=0)
    padded = ((counts + tm - 1) // tm) * tm
    ends = jnp.cumsum(padded)
    slot = jnp.sum(onehot * (ends - padded)[None, :], axis=1) + rank
    n_rows = 2 * n_tok + n_experts * tm
    n_tiles = n_rows // tm
    tok = jnp.tile(jnp.arange(n_tok, dtype=jnp.int32), 2)
    row_token = jnp.zeros((n_rows,), jnp.int32).at[slot].set(tok)
    starts = jnp.arange(n_tiles, dtype=jnp.int32) * tm
    tile_expert = jnp.minimum(jnp.sum((starts[:, None] >= ends[None, :]).astype(jnp.int32), axis=1), n_experts - 1)
    tile_valid = (starts < ends[-1]).astype(jnp.int32)
    return slot, row_token, tile_expert, tile_valid


def _row_copy(src_ref, dst_ref, sem, src_row, dst_row):
    return pltpu.make_async_copy(src_ref.at[pl.ds(src_row, 1)], dst_ref.at[pl.ds(dst_row, 1)], sem)


def _gather_body(idx_ref, src_ref, dst_ref, sem, *, rows):
    def issue(r, carry):
        _row_copy(src_ref, dst_ref, sem, idx_ref[0, 0, r], r).start()
        return carry

    lax.fori_loop(0, rows, issue, 0, unroll=8)
    pltpu.make_async_copy(src_ref.at[pl.ds(0, rows)], dst_ref, sem).wait()


def _row_gather(src, idx):
    m = idx.shape[0]
    d = src.shape[1]
    rows = _pick(m, GATHER_ROWS, V7X_SUBLANES)
    return pl.pallas_call(
        functools.partial(_gather_body, rows=rows),
        grid=(m // rows,),
        in_specs=[
            pl.BlockSpec((1, 1, rows), lambda i: (i, 0, 0), memory_space=pltpu.SMEM),
            pl.BlockSpec(memory_space=pl.ANY),
        ],
        out_specs=pl.BlockSpec((rows, d), lambda i: (i, 0)),
        out_shape=jax.ShapeDtypeStruct((m, d), src.dtype),
        scratch_shapes=[pltpu.SemaphoreType.DMA(())],
        compiler_params=_cparams(("arbitrary",), 24),
        name="moe_row_gather",
    )(idx.reshape(m // rows, 1, rows), src)


def _experts_body(te_ref, tv_ref, hs_ref, w1_ref, w3_ref, w2_ref, o_ref, hb_ref):
    i = pl.program_id(0)
    f = pl.program_id(1)
    valid = tv_ref[i] != 0

    @pl.when(valid & (f == 0))
    def _():
        hb_ref[...] = hs_ref[...].astype(BF16)

    @pl.when(f == 0)
    def _():
        o_ref[...] = jnp.zeros_like(o_ref)

    @pl.when(valid)
    def _():
        h = hb_ref[...]
        act = _silu(_dot(h, w1_ref[0])) * _dot(h, w3_ref[0])
        o_ref[...] += _dot(act.astype(BF16), w2_ref[0])


def _experts(hs, tile_expert, tile_valid, w1, w3, w2):
    p, d = hs.shape
    ff = w1.shape[2]
    tm = MOE_TILE
    tf = _pick(ff, 512, V7X_LANES)
    nf = ff // tf
    f_eff = lambda i, f, tv: jnp.where(tv[i] != 0, f, nf - 1)
    grid_spec = pltpu.PrefetchScalarGridSpec(
        num_scalar_prefetch=2,
        grid=(p // tm, nf),
        in_specs=[
            pl.BlockSpec((tm, d), lambda i, f, te, tv: (i, 0)),
            pl.BlockSpec((1, d, tf), lambda i, f, te, tv: (te[i], 0, f_eff(i, f, tv))),
            pl.BlockSpec((1, d, tf), lambda i, f, te, tv: (te[i], 0, f_eff(i, f, tv))),
            pl.BlockSpec((1, tf, d), lambda i, f, te, tv: (te[i], f_eff(i, f, tv), 0)),
        ],
        out_specs=pl.BlockSpec((tm, d), lambda i, f, te, tv: (i, 0)),
        scratch_shapes=[pltpu.VMEM((tm, d), BF16)],
    )
    return pl.pallas_call(
        _experts_body,
        grid_spec=grid_spec,
        out_shape=jax.ShapeDtypeStruct((p, d), F32),
        compiler_params=_cparams(("arbitrary", "arbitrary"), 48),
        name="routed_experts",
    )(tile_expert, tile_valid, hs, w1, w3, w2)


def _combine_body(i0_ref, i1_ref, x_ref, gate_ref, info_ref, ys_ref, o_ref, buf_ref, sem, *, rows):
    def issue(r, carry):
        _row_copy(ys_ref, buf_ref.at[0], sem, i0_ref[0, 0, r], r).start()
        _row_copy(ys_ref, buf_ref.at[1], sem, i1_ref[0, 0, r], r).start()
        return carry

    lax.fori_loop(0, rows, issue, 0, unroll=8)
    for k in range(2):
        pltpu.make_async_copy(ys_ref.at[pl.ds(0, rows)], buf_ref.at[k], sem).wait()
    info = info_ref[...]
    y = info[:, 2:3] * buf_ref[0] + info[:, 3:4] * buf_ref[1]
    o_ref[...] = x_ref[...] + gate_ref[0] * y


def _combine(x, gate, info, ys, slot):
    b, l, d = x.shape
    n_tok = b * l
    rows = _pick(l, GATHER_ROWS, V7X_SUBLANES)
    per_batch = gate.shape[0] > 1
    blocks_per_seq = l // rows
    gate_idx = (lambda i: (i // blocks_per_seq, 0, 0)) if per_batch else (lambda i: (0, 0, 0))
    idx = slot.reshape(2, n_tok // rows, 1, rows)
    out = pl.pallas_call(
        functools.partial(_combine_body, rows=rows),
        grid=(n_tok // rows,),
        in_specs=[
            pl.BlockSpec((1, 1, rows), lambda i: (i, 0, 0), memory_space=pltpu.SMEM),
            pl.BlockSpec((1, 1, rows), lambda i: (i, 0, 0), memory_space=pltpu.SMEM),
            pl.BlockSpec((rows, d), lambda i: (i, 0)),
            pl.BlockSpec((1, 1, d), gate_idx),
            pl.BlockSpec((rows, V7X_LANES), lambda i: (i, 0)),
            pl.BlockSpec(memory_space=pl.ANY),
        ],
        out_specs=pl.BlockSpec((rows, d), lambda i: (i, 0)),
        out_shape=jax.ShapeDtypeStruct((n_tok, d), F32),
        scratch_shapes=[pltpu.VMEM((2, rows, d), F32), pltpu.SemaphoreType.DMA(())],
        compiler_params=_cparams(("arbitrary",), 32),
        name="moe_combine",
    )(idx[0], idx[1], x.reshape(n_tok, d), gate, info, ys)
    return out.reshape(b, l, d)


def _moe(x, g, sc, sh, gate, router, w1, w3, w2):
    b, l, d = x.shape
    n_experts = w1.shape[0]
    h, info = _route(x, g, sc, sh, router, n_experts)
    info = info.reshape(b * l, V7X_LANES)
    slot, row_token, tile_expert, tile_valid = _route_plan(info, n_experts, MOE_TILE)
    hs = _row_gather(h.reshape(b * l, d), row_token)
    ys = _experts(hs, tile_expert, tile_valid, w1, w3, w2)
    return _combine(x, gate, info, ys, slot)


HGRN_BLOCK = 128
HGRN_LEVEL_HALF_WIDTHS = (64, 32, 16, 8)
HGRN_DIAG = 8


@functools.lru_cache(maxsize=None)
def _hgrn_consts(reverse):
    n = HGRN_BLOCK
    t = np.arange(n)[:, None]
    s = np.arange(n)[None, :]
    before = (s >= t) if reverse else (s <= t)
    masks = []
    for w in HGRN_LEVEL_HALF_WIDTHS:
        same = (t // (2 * w)) == (s // (2 * w))
        t_late = (t % (2 * w)) >= w
        s_late = (s % (2 * w)) >= w
        masks.append(same & ~t_late & s_late if reverse else same & t_late & ~s_late)
    masks.append(((t // HGRN_DIAG) == (s // HGRN_DIAG)) & before)
    row_m = np.arange(HGRN_DIAG * n)[:, None] // n
    wsel = (np.arange(n)[None, :] % HGRN_DIAG) == row_m
    return (before.astype(np.float32), np.stack(masks).astype(np.float32), wsel.astype(np.float32))


def _group_row(x, n, r):
    x3 = x.reshape(HGRN_BLOCK // n, n, x.shape[-1])
    return jnp.broadcast_to(x3[:, r:r + 1, :], x3.shape).reshape(x.shape)


def _hgrn_head(zq, zf, v, lb, st, lc, mask_ref, wsel, reverse):
    sig = jax.nn.sigmoid(zf.astype(F32))
    g = jnp.log2(lb + (1.0 - lb) * sig)
    kk = (1.0 - lb) * (1.0 - sig)
    q = _silu(zq.astype(F32))
    g_hi = g.astype(BF16)
    rem = g - g_hi.astype(F32)
    g_mid = rem.astype(BF16)
    g_lo = (rem - g_mid.astype(F32)).astype(BF16)
    b3 = _dot(lc, jnp.concatenate([g_hi, g_mid, g_lo], axis=1))
    n = HGRN_BLOCK
    b = b3[:, :n] + (b3[:, n:2 * n] + b3[:, 2 * n:])
    a = None
    for li, w in enumerate(HGRN_LEVEL_HALF_WIDTHS):
        e = jnp.exp2(-jnp.abs(b - _group_row(b, 2 * w, w if reverse else w - 1)))
        al = _dot_nt((q * e).astype(BF16), (kk * e).astype(BF16)) * mask_ref[li]
        a = al if a is None else a + al
    vals = []
    for m in range(HGRN_DIAG):
        e = jnp.exp2(jnp.minimum(b - _group_row(b, HGRN_DIAG, m), 0.0))
        vals.append((q * _group_row(kk, HGRN_DIAG, m) * e).astype(BF16))
    a = a + _dot(jnp.concatenate(vals, axis=1), wsel) * mask_ref[len(HGRN_LEVEL_HALF_WIDTHS)]
    end = 0 if reverse else n - 1
    b_end = b[end:end + 1, :]
    o = _dot(a.astype(BF16), v) + _dot_nt((q * jnp.exp2(b)).astype(BF16), st.astype(BF16))
    k_end = (kk * jnp.exp2(b_end - b)).astype(BF16)
    return o, st * jnp.exp2(b_end) + _dot_tn(v, k_end)


def _hgrn_body(q_ref, f_ref, v_ref, lb_ref, s0_ref, lc_ref, mask_ref, wsel_ref, o_ref, st_ref, *, reverse, n_heads):
    @pl.when(pl.program_id(1) == 0)
    def _():
        st_ref[...] = s0_ref[...]

    def head(h, carry):
        o, st_new = _hgrn_head(q_ref[0, h], f_ref[0, h], v_ref[0, h], lb_ref[h], st_ref[0, h],
                               lc_ref[...], mask_ref, wsel_ref[...], reverse)
        o_ref[0, h] = o.astype(o_ref.dtype)
        st_ref[0, h] = st_new
        return carry

    lax.fori_loop(0, n_heads, head, 0, unroll=4 if n_heads % 4 == 0 else 1)


def _hgrn_scan(proj, lb, s0, n_heads, reverse):
    bsz, _, l, _ = proj.shape
    nb = l // HGRN_BLOCK
    lc, masks, wsel = _hgrn_consts(reverse)
    blk = (lambda i: nb - 1 - i) if reverse else (lambda i: i)
    f_seg = 2 if reverse else 1
    tile = (1, n_heads, HGRN_BLOCK, HEAD_DIM)
    full = lambda shape: pl.BlockSpec(shape, lambda b, i: (0,) * len(shape))
    return pl.pallas_call(
        functools.partial(_hgrn_body, reverse=reverse, n_heads=n_heads),
        grid=(bsz, nb),
        in_specs=[
            pl.BlockSpec(tile, lambda b, i: (b, 0, blk(i), 0)),
            pl.BlockSpec(tile, lambda b, i: (b, f_seg, blk(i), 0)),
            pl.BlockSpec(tile, lambda b, i: (b, 3, blk(i), 0)),
            full((n_heads, 1, HEAD_DIM)),
            pl.BlockSpec((1, n_heads, HEAD_DIM, HEAD_DIM), lambda b, i: (b, 0, 0, 0)),
            full((HGRN_BLOCK, HGRN_BLOCK)),
            full((len(HGRN_LEVEL_HALF_WIDTHS) + 1, HGRN_BLOCK, HGRN_BLOCK)),
            full((HGRN_DIAG * HGRN_BLOCK, HGRN_BLOCK)),
        ],
        out_specs=[
            pl.BlockSpec(tile, lambda b, i: (b, 0, blk(i), 0)),
            pl.BlockSpec((1, n_heads, HEAD_DIM, HEAD_DIM), lambda b, i: (b, 0, 0, 0)),
        ],
        out_shape=[
            jax.ShapeDtypeStruct((bsz, n_heads, l, HEAD_DIM), BF16),
            jax.ShapeDtypeStruct((bsz, n_heads, HEAD_DIM, HEAD_DIM), F32),
        ],
        compiler_params=_cparams(("parallel", "arbitrary"), 32),
        name="hgrn2_scan_bwd" if reverse else "hgrn2_scan_fwd",
    )(proj, proj, proj, lb, s0, jnp.asarray(lc, BF16), jnp.asarray(masks), jnp.asarray(wsel, BF16))


CONV_HALO = 16


def _conv_finish_body(of_ref, ob_ref, og_ref, a_ref, g_ref, ap_ref, gp_ref, an_ref, gn_ref, onorm_ref, w_ref,
                      cb_ref, lng_ref, lnb_ref, oa_ref, oc_ref, ext_ref, y_ref, *, n_heads, n_cc, k_taps):
    i = pl.program_id(1)
    last = pl.num_programs(1) - 1
    tt = a_ref.shape[2]
    lanes = V7X_LANES
    for h in range(n_heads):
        o = of_ref[0, h].astype(F32) + ob_ref[0, h].astype(F32)
        y = o * lax.rsqrt(jnp.mean(o * o, axis=-1, keepdims=True) + EPS) * onorm_ref[h]
        oa_ref[0, :, h * lanes:(h + 1) * lanes] = (y * _silu(og_ref[0, h].astype(F32))).astype(BF16)

    glu = lambda a, g: a.astype(F32) * jax.nn.sigmoid(g.astype(F32))
    first_tap = CONV_HALO - k_taps // 2
    tot = jnp.zeros((tt, lanes), F32)
    for c in range(n_cc):
        ext_ref[c, 0:CONV_HALO] = jnp.where(i > 0, glu(ap_ref[0, c], gp_ref[0, c]), 0.0)
        ext_ref[c, CONV_HALO:CONV_HALO + tt] = glu(a_ref[0, c], g_ref[0, c])
        ext_ref[c, CONV_HALO + tt:2 * CONV_HALO + tt] = jnp.where(i < last, glu(an_ref[0, c], gn_ref[0, c]), 0.0)
        acc = jnp.zeros((tt, lanes), F32)
        for j in range(k_taps):
            acc = acc + w_ref[c, j:j + 1, :] * ext_ref[c, first_tap + j:first_tap + j + tt, :]
        y = acc + cb_ref[c]
        y_ref[c] = y
        tot = tot + y
    inv_n = 1.0 / (n_cc * lanes)
    mu = jnp.sum(tot, axis=-1, keepdims=True) * inv_n
    sq = jnp.zeros((tt, lanes), F32)
    for c in range(n_cc):
        d = y_ref[c] - mu
        sq = sq + d * d
    rstd = lax.rsqrt(jnp.sum(sq, axis=-1, keepdims=True) * inv_n + EPS)
    for c in range(n_cc):
        un = (y_ref[c] - mu) * rstd * lng_ref[c] + lnb_ref[c]
        oc_ref[0, :, c * lanes:(c + 1) * lanes] = _silu(un).astype(BF16)


def _conv_finish(proj, o_f, o_b, onorm_g, dw_w, dw_b, ln_g, ln_b):
    bsz, _, l, lanes = proj.shape
    n_heads = o_f.shape[1]
    k_taps, bw = dw_w.shape
    n_cc = bw // lanes
    assert n_cc == n_heads and k_taps // 2 < CONV_HALO
    tt = _pick(l, 256, CONV_HALO)
    hb = tt // CONV_HALO
    n_halo = l // CONV_HALO
    seg_a = 5 * n_heads // n_cc
    tile = lambda seg: pl.BlockSpec((1, n_heads, tt, lanes), lambda b, i: (b, seg, i, 0))
    halo_prev = lambda seg: pl.BlockSpec((1, n_cc, CONV_HALO, lanes),
                                         lambda b, i: (b, seg, jnp.maximum(i * hb - 1, 0), 0))
    halo_next = lambda seg: pl.BlockSpec((1, n_cc, CONV_HALO, lanes),
                                         lambda b, i: (b, seg, jnp.minimum((i + 1) * hb, n_halo - 1), 0))
    full = lambda shape: pl.BlockSpec(shape, lambda b, i: (0,) * len(shape))
    per_chunk = lambda p: p.reshape(n_cc, 1, lanes)
    w = jnp.transpose(dw_w.reshape(k_taps, n_cc, lanes), (1, 0, 2))
    return pl.pallas_call(
        functools.partial(_conv_finish_body, n_heads=n_heads, n_cc=n_cc, k_taps=k_taps),
        grid=(bsz, l // tt),
        in_specs=[
            pl.BlockSpec((1, n_heads, tt, lanes), lambda b, i: (b, 0, i, 0)),
            pl.BlockSpec((1, n_heads, tt, lanes), lambda b, i: (b, 0, i, 0)),
            tile(4), tile(seg_a), tile(seg_a + 1),
            halo_prev(seg_a), halo_prev(seg_a + 1), halo_next(seg_a), halo_next(seg_a + 1),
            full((n_heads, 1, lanes)), full((n_cc, k_taps, lanes)),
            full((n_cc, 1, lanes)), full((n_cc, 1, lanes)), full((n_cc, 1, lanes)),
        ],
        out_specs=[
            pl.BlockSpec((1, tt, n_heads * lanes), lambda b, i: (b, i, 0)),
            pl.BlockSpec((1, tt, bw), lambda b, i: (b, i, 0)),
        ],
        out_shape=[
            jax.ShapeDtypeStruct((bsz, l, n_heads * lanes), BF16),
            jax.ShapeDtypeStruct((bsz, l, bw), BF16),
        ],
        scratch_shapes=[pltpu.VMEM((n_cc, tt + 2 * CONV_HALO, lanes), F32), pltpu.VMEM((n_cc, tt, lanes), F32)],
        compiler_params=_cparams(("parallel", "parallel"), 32),
        name="hgrn2_gate_conv_module",
    )(o_f, o_b, proj, proj, proj, proj, proj, proj, proj, onorm_g.reshape(n_heads, 1, lanes), w,
      per_chunk(dw_b), per_chunk(ln_g), per_chunk(ln_b))


def _rope_tables(n_tokens, head_dim, scale, rotate):
    lane = np.arange(V7X_LANES) % head_dim
    q4 = head_dim // 4
    col = (lane // (2 * q4)) * q4 + lane % q4
    sign = np.where(lane % (2 * q4) < q4, -1.0, 1.0).astype(np.float32)
    if not rotate:
        return (jnp.full((n_tokens, V7X_LANES), scale, F32), jnp.zeros((n_tokens, V7X_LANES), F32))
    t = jnp.arange(n_tokens)
    axis_dim = head_dim // 2
    inv_freq = ROPE_THETA ** (-jnp.arange(0, axis_dim, 2, dtype=F32) / axis_dim)
    ang = jnp.concatenate([(t // GRID_W).astype(F32)[:, None] * inv_freq,
                           (t % GRID_W).astype(F32)[:, None] * inv_freq], axis=-1)
    return (jnp.cos(ang)[:, col] * scale, jnp.sin(ang)[:, col] * (sign * scale))


def _qk_prep_one(x, gain, gmat, cos, sin, half_w):
    x = x.astype(F32)
    x2 = x * x
    hi = x2.astype(BF16)
    lo = (x2 - hi.astype(F32)).astype(BF16)
    y = x * lax.rsqrt(_dot(hi, gmat) + _dot(lo, gmat) + EPS) * gain
    lane = lax.broadcasted_iota(jnp.int32, y.shape, 1)
    partner = jnp.where(lane % (2 * half_w) < half_w,
                        pltpu.roll(y, V7X_LANES - half_w, 1), pltpu.roll(y, half_w, 1))
    return (y * cos + partner * sin).astype(BF16)


def _qk_prep_body(cq_ref, ck_ref, dq_ref, dk_ref, tab_ref, gain_ref, gmat_ref, o_ref, *, n_heads, n_kv):
    g64, g128 = gmat_ref[0], gmat_ref[1]
    d_half, g_half = HEAD_DIM // 8, HEAD_DIM // 4
    for h in range(n_heads):
        o_ref[0, h] = _qk_prep_one(cq_ref[0, h], gain_ref[0], g64, tab_ref[0], tab_ref[1], d_half)
        o_ref[0, n_heads + h] = _qk_prep_one(ck_ref[0, h], gain_ref[1], g64, tab_ref[2], tab_ref[3], d_half)
        o_ref[0, 2 * n_heads + h] = _qk_prep_one(dq_ref[0, h], gain_ref[2], g128, tab_ref[4], tab_ref[5], g_half)
    for h in range(n_kv):
        o_ref[0, 3 * n_heads + h] = _qk_prep_one(dk_ref[0, h], gain_ref[3], g128, tab_ref[6], tab_ref[7], g_half)


def _qk_prep(proj, gains, n_heads, n_kv, rotate):
    bsz, _, l, lanes = proj.shape
    diff_d = HEAD_DIM // 2
    log2e = math.log2(math.e)
    tabs = jnp.stack(_rope_tables(l, diff_d, diff_d ** -0.5 * log2e, rotate) + _rope_tables(l, diff_d, 1.0, rotate)
                     + _rope_tables(l, HEAD_DIM, HEAD_DIM ** -0.5 * log2e, rotate) + _rope_tables(l, HEAD_DIM, 1.0, rotate))
    lane = np.arange(lanes)
    gmat = np.stack([(lane[:, None] // diff_d == lane[None, :] // diff_d) / diff_d, np.full((lanes, lanes), 1.0 / lanes)])
    tt = _pick(l, 256, V7X_SUBLANES)
    tile = lambda n, idx: pl.BlockSpec((1, n, tt, lanes), lambda b, i: (b, idx, i, 0))
    n_out = 3 * n_heads + n_kv
    return pl.pallas_call(
        functools.partial(_qk_prep_body, n_heads=n_heads, n_kv=n_kv),
        grid=(bsz, l // tt),
        in_specs=[
            tile(n_heads, 0), tile(n_heads, 1), tile(n_heads, 3), tile(n_kv, 4 * n_heads // n_kv),
            pl.BlockSpec((8, tt, lanes), lambda b, i: (0, i, 0)),
            pl.BlockSpec((4, 1, lanes), lambda b, i: (0, 0, 0)),
            pl.BlockSpec((2, lanes, lanes), lambda b, i: (0, 0, 0)),
        ],
        out_specs=pl.BlockSpec((1, n_out, tt, lanes), lambda b, i: (b, 0, i, 0)),
        out_shape=jax.ShapeDtypeStruct((bsz, n_out, l, lanes), BF16),
        compiler_params=_cparams(("parallel", "parallel"), 32),
        name="qk_norm_rope",
    )(proj, proj, proj, proj, tabs, gains, jnp.asarray(gmat, BF16))


def _attn_body(lam_ref, subg_ref, q_ref, *refs, n_stack, diff, lam_init, chunks):
    n_src = len(chunks)
    kv_refs = refs[:2 * n_src]
    o_ref, qs_ref, m_ref, acc_ref = refs[2 * n_src:]
    tq = q_ref.shape[2]
    if diff:
        q = q_ref[0, 0]
        lane = lax.broadcasted_iota(jnp.int32, q.shape, 1)
        zero = jnp.zeros_like(q)
        qs_ref[0:tq] = jnp.where(lane < HEAD_DIM // 2, q, zero)
        qs_ref[tq:2 * tq] = jnp.where(lane >= HEAD_DIM // 2, q, zero)
    else:
        for g in range(n_stack):
            qs_ref[g * tq:(g + 1) * tq] = q_ref[0, g]
    m_ref[...] = jnp.full(m_ref.shape, NEG_BIG, F32)
    acc_ref[...] = jnp.zeros(acc_ref.shape, F32)
    lanes = V7X_LANES

    def visit(k, v):
        s = _dot_nt(qs_ref[...], k)
        n_t = s.shape[1] // lanes
        tiles = [s[:, t * lanes:(t + 1) * lanes] for t in range(n_t)]
        mt = tiles[0]
        for t in range(1, n_t):
            mt = jnp.maximum(mt, tiles[t])
        m_prev = m_ref[...]
        m_new = jnp.maximum(m_prev, jnp.max(mt, axis=-1, keepdims=True))
        alpha = jnp.exp2(m_prev - m_new)
        p = jnp.concatenate([jnp.exp2(t - m_new).astype(BF16) for t in tiles], axis=1)
        v_ext = jnp.concatenate([v, jnp.ones(v.shape, BF16)], axis=1)
        acc_ref[...] = jnp.concatenate([alpha, alpha], axis=1) * acc_ref[...] + _dot(p, v_ext)
        m_ref[...] = m_new

    for src, (tk, n_chunks) in enumerate(chunks):
        k_ref, v_ref = kv_refs[2 * src], kv_refs[2 * src + 1]
        if n_chunks == 1:
            visit(k_ref[0, 0], v_ref[0, 0])
        else:
            def step(ci, carry, k_ref=k_ref, v_ref=v_ref, tk=tk):
                rows = pl.ds(pl.multiple_of(ci * tk, tk), tk)
                visit(k_ref[0, 0, rows, :], v_ref[0, 0, rows, :])
                return carry
            lax.fori_loop(0, n_chunks, step, 0, unroll=8 if n_chunks % 8 == 0 else 1)

    o = acc_ref[:, 0:lanes] / acc_ref[:, lanes:2 * lanes]
    if diff:
        lp = lam_ref[...]
        lam = (jnp.exp(jnp.sum(lp[0:1] * lp[1:2], axis=-1, keepdims=True))
               - jnp.exp(jnp.sum(lp[2:3] * lp[3:4], axis=-1, keepdims=True)) + lam_init)
        d = o[0:tq] - lam * o[tq:2 * tq]
        y = d * lax.rsqrt(jnp.mean(d * d, axis=-1, keepdims=True) + EPS) * subg_ref[...]
        o_ref[0] = (y * (1.0 - lam_init)).astype(BF16)
    else:
        for g in range(n_stack):
            o_ref[0, :, g * HEAD_DIM:(g + 1) * HEAD_DIM] = o[g * tq:(g + 1) * tq].astype(BF16)


def _attention(qk_q, kv_srcs, lam_p, subln_g, lam_init, n_heads, n_kv, diff):
    bsz, _, lq, lanes = qk_q.shape
    group = n_heads // n_kv
    n_stack = 2 if diff else group
    n_units = n_heads if diff else n_kv
    tq = _pick(lq, 1024 // n_stack, V7X_SUBLANES)
    rows = n_stack * tq
    if diff:
        q_spec = pl.BlockSpec((1, 1, tq, lanes), lambda b, u, i: (b, u, i, 0))
        k_chunk = lambda u: n_heads + u
        v_chunk = lambda u: 2 * n_heads + u
        out_w = lanes
    else:
        q_spec = pl.BlockSpec((1, group, tq, lanes), lambda b, u, i: (b, 2 * n_heads // group + u, i, 0))
        k_chunk = lambda u: 3 * n_heads + u
        v_chunk = lambda u: 4 * n_heads + n_kv + u
        out_w = group * lanes
    in_specs = [pl.BlockSpec((4, lanes), lambda b, u, i: (0, 0)), pl.BlockSpec((1, lanes), lambda b, u, i: (0, 0)), q_spec]
    args = [lam_p, subln_g.reshape(1, lanes), qk_q]
    chunks = []
    for qk_k, proj in kv_srcs:
        lk = qk_k.shape[2]
        tk = _pick(lk, 512, V7X_SUBLANES)
        chunks.append((tk, lk // tk))
        in_specs.append(pl.BlockSpec((1, 1, lk, lanes), lambda b, u, i: (b, k_chunk(u), 0, 0)))
        in_specs.append(pl.BlockSpec((1, 1, lk, lanes), lambda b, u, i: (b, v_chunk(u), 0, 0)))
        args += [qk_k, proj]
    return pl.pallas_call(
        functools.partial(_attn_body, n_stack=n_stack, diff=diff, lam_init=lam_init, chunks=tuple(chunks)),
        grid=(bsz, n_units, lq // tq),
        in_specs=in_specs,
        out_specs=pl.BlockSpec((1, tq, out_w), lambda b, u, i: (b, i, u)),
        out_shape=jax.ShapeDtypeStruct((bsz, lq, n_heads * lanes), BF16),
        scratch_shapes=[pltpu.VMEM((rows, lanes), BF16), pltpu.VMEM((rows, lanes), F32),
                        pltpu.VMEM((rows, 2 * lanes), F32)],
        compiler_params=_cparams(("parallel", "parallel", "arbitrary"), 40),
        name="diff_attention" if diff else "gqa_attention",
    )(*args)


def kernel(x, c, ctx, c_ctx, ada_w, ada_b, norm_mix_g, norm_ffn_g, w_out, ab_w_in, hgrn_lb_logits, hgrn_onorm_g,
           conv_dw_w, conv_dw_b, conv_ln_g, conv_ln_b, cd_w_in, diff_qnorm_g, diff_knorm_g, diff_lambda, diff_subln_g,
           gqa_qnorm_g, gqa_knorm_g, ffn_w1, ffn_w3, ffn_w2, moe_router, moe_w1, moe_w3, moe_w2):
    bsz, _, d = x.shape
    depth = ada_w.shape[0]
    lanes = V7X_LANES
    bf = lambda t: t.astype(BF16)
    n_heads = (d // 2) // HEAD_DIM
    n_kv = (cd_w_in.shape[2] - 4 * n_heads * HEAD_DIM) // (2 * HEAD_DIM)
    n_experts = moe_router.shape[2]

    rows = -(-(bsz + 1) // V7X_SUBLANES) * V7X_SUBLANES
    cond = jnp.zeros((rows, d), F32).at[:bsz].set(c).at[bsz].set(c_ctx)
    mods = _ada_all(cond, ada_w, ada_b)
    lbs = jnp.cumsum(jax.nn.softmax(hgrn_lb_logits.astype(F32), axis=0), axis=0)
    lbs = lbs - lbs[:1]

    for li in range(depth):
        j = li // 2
        need_ctx = li < depth - 1
        sh1, sc1, g1, sh2, sc2, g2 = [t.reshape(bsz, 1, d) for t in jnp.split(mods[li, :bsz], 6, axis=-1)]
        csh1, csc1, cg1, csh2, csc2, cg2 = [t.reshape(1, 1, d) for t in jnp.split(mods[li, bsz:bsz + 1], 6, axis=-1)]
        wo = bf(w_out[li])
        if li % 2 == 0:
            w_in = bf(ab_w_in[j])
            px = _norm_mm(x, norm_mix_g[li], sc1, sh1, w_in)
            pc = _norm_mm(ctx, norm_mix_g[li], csc1, csh1, w_in)
            lb_f = lbs[j, 0].reshape(n_heads, 1, HEAD_DIM)
            lb_b = lbs[j, 1].reshape(n_heads, 1, HEAD_DIM)
            s0 = jnp.zeros((bsz, n_heads, HEAD_DIM, HEAD_DIM), F32)
            ocf, s_f = _hgrn_scan(pc, lb_f, s0, n_heads, False)
            ocb, s_b = _hgrn_scan(pc, lb_b, s0, n_heads, True)
            oxf, _ = _hgrn_scan(px, lb_f, s_f, n_heads, False)
            oxb, _ = _hgrn_scan(px, lb_b, s_b, n_heads, True)
            conv_args = (hgrn_onorm_g[j], conv_dw_w[j], conv_dw_b[j], conv_ln_g[j], conv_ln_b[j])
            mix_x = _conv_finish(px, oxf, oxb, *conv_args)
            mix_c = _conv_finish(pc, ocf, ocb, *conv_args) if need_ctx else None
        else:
            w_in = bf(cd_w_in[j])
            px = _norm_mm(x, norm_mix_g[li], sc1, sh1, w_in)
            pc = _norm_mm(ctx, norm_mix_g[li], csc1, csh1, w_in)
            tile2 = lambda g: jnp.tile(g, lanes // g.shape[0])
            gains = jnp.stack([tile2(diff_qnorm_g[j]), tile2(diff_knorm_g[j]), gqa_qnorm_g[j], gqa_knorm_g[j]])
            gains = gains.reshape(4, 1, lanes).astype(F32)
            qx = _qk_prep(px, gains, n_heads, n_kv, True)
            qc = _qk_prep(pc, gains, n_heads, n_kv, False)
            lam_init = 0.8 - 0.6 * math.exp(-0.3 * li)
            lam_p = jnp.zeros((4, lanes), F32).at[:, :diff_lambda.shape[2]].set(diff_lambda[j])
            attn = lambda q, srcs, diff: _attention(q, srcs, lam_p, diff_subln_g[j], lam_init, n_heads, n_kv, diff)
            srcs_x = [(qc, pc), (qx, px)]
            mix_x = (attn(qx, srcs_x, True), attn(qx, srcs_x, False))
            mix_c = (attn(qc, [(qc, pc)], True), attn(qc, [(qc, pc)], False)) if need_ctx else None
        x = _mm_resid(mix_x[0], mix_x[1], wo, x, g1)
        if need_ctx:
            ctx = _mm_resid(mix_c[0], mix_c[1], wo, ctx, cg1)

        if li % 2 == 0:
            mixer = functools.partial(_ffn, w1=bf(ffn_w1[j]), w3=bf(ffn_w3[j]), w2=bf(ffn_w2[j]))
        else:
            router = jnp.zeros((d, lanes), F32).at[:, :n_experts].set(moe_router[j])
            mixer = functools.partial(_moe, router=router, w1=bf(moe_w1[j]), w3=bf(moe_w3[j]), w2=bf(moe_w2[j]))
        x = mixer(x, norm_ffn_g[li], sc2, sh2, g2)
        if need_ctx:
            ctx = mixer(ctx, norm_ffn_g[li], csc2, csh2, cg2)
    return x
```

```python
import functools
import math

import numpy as np
import jax
import jax.numpy as jnp
from jax import lax
from jax.experimental import pallas as pl
from jax.experimental.pallas import tpu as pltpu

F32 = jnp.float32
BF16 = jnp.bfloat16
EPS = 1e-6
HEAD_DIM = 128
GRID_W = 64
ROPE_THETA = 10000.0
TOP_K = 2
NEG_BIG = -1e30
V7X_LANES = 128
V7X_SUBLANES = 8
MIB = 1 << 20


def _cparams(semantics, vmem_mib):
    return pltpu.CompilerParams(dimension_semantics=semantics, vmem_limit_bytes=int(vmem_mib * MIB))


def _pick(n, pref, mult):
    if n <= pref:
        return n
    t = (pref // mult) * mult
    while t >= mult:
        if n % t == 0:
            return t
        t -= mult
    return n


def _dot(a, b):
    return jnp.dot(a, b, preferred_element_type=F32)


def _dot_nt(a, b):
    return lax.dot_general(a, b, (((1,), (1,)), ((), ())), preferred_element_type=F32)


def _dot_tn(a, b):
    return lax.dot_general(a, b, (((0,), (0,)), ((), ())), preferred_element_type=F32)


def _silu(x):
    return x * jax.nn.sigmoid(x)


def _norm_modulate(x, g, sc, sh):
    ms = jnp.mean(x * x, axis=-1, keepdims=True)
    return (x * lax.rsqrt(ms + EPS)) * g * (1.0 + sc) + sh


NORM_ROWS = 16


def _norm_modulate_rows(x_ref, g, sc, sh, out_ref):
    gs = g * (1.0 + sc)

    def piece(r, carry):
        rows = pl.ds(pl.multiple_of(r * NORM_ROWS, NORM_ROWS), NORM_ROWS)
        x = x_ref[0, rows, :]
        ms = jnp.mean(x * x, axis=-1, keepdims=True)
        out_ref[rows, :] = (x * lax.rsqrt(ms + EPS) * gs + sh).astype(out_ref.dtype)
        return carry

    lax.fori_loop(0, out_ref.shape[0] // NORM_ROWS, piece, 0, unroll=8)


def _cast_body(w_ref, o_ref):
    o_ref[...] = w_ref[0].astype(BF16)


def _cast_bf16(w, lead):
    n, c = w.shape[0], w.shape[-1]
    r = math.prod(w.shape[1:-1])
    tr = _pick(r, 256, 2 * V7X_SUBLANES)
    out = pl.pallas_call(
        _cast_body,
        grid=(r // tr,),
        in_specs=[pl.BlockSpec((1, tr, c), lambda i: (lead, i, 0))],
        out_specs=pl.BlockSpec((tr, c), lambda i: (i, 0)),
        out_shape=jax.ShapeDtypeStruct((r, c), BF16),
        compiler_params=_cparams(("parallel",), 40),
        name="weights_to_bf16",
    )(w.reshape(n, r, c))
    return out.reshape(w.shape[1:])


def _ada_body(c_ref, w_ref, b_ref, o_ref):
    cond = c_ref[...]
    a = _silu(cond).astype(BF16)
    o_ref[0] = _dot(a, w_ref[0].astype(BF16)) + b_ref[0]


def _ada_all(cond, ada_w, ada_b):
    depth, d, n = ada_w.shape
    r = cond.shape[0]
    tn = _pick(n, 1024, V7X_LANES)
    return pl.pallas_call(
        _ada_body,
        grid=(depth, n // tn),
        in_specs=[
            pl.BlockSpec((r, d), lambda l, j: (0, 0)),
            pl.BlockSpec((1, d, tn), lambda l, j: (l, 0, j)),
            pl.BlockSpec((1, 1, tn), lambda l, j: (l, 0, j)),
        ],
        out_specs=pl.BlockSpec((1, r, tn), lambda l, j: (l, 0, j)),
        out_shape=jax.ShapeDtypeStruct((depth, r, n), F32),
        compiler_params=_cparams(("arbitrary", "arbitrary"), 40),
        name="ada_modulation",
    )(cond, ada_w, ada_b.reshape(depth, 1, n))


def _norm_mm_body(x_ref, g_ref, sc_ref, sh_ref, w_ref, o_ref, h_ref):
    @pl.when(pl.program_id(2) == 0)
    def _():
        _norm_modulate_rows(x_ref, g_ref[...], sc_ref[0], sh_ref[0], h_ref)

    res = _dot(h_ref[...], w_ref[...]).astype(o_ref.dtype)
    for c in range(o_ref.shape[1]):
        o_ref[0, c] = res[:, c * V7X_LANES:(c + 1) * V7X_LANES]


def _norm_mm(x, g, sc, sh, w):
    b, l, d = x.shape
    n = w.shape[1]
    tm = _pick(l, 1024, V7X_SUBLANES)
    tn = _pick(n, 1024, 2 * V7X_LANES)
    nc = tn // V7X_LANES
    per_batch = sc.shape[0] > 1
    mod_idx = (lambda bi, i, j: (bi, 0, 0)) if per_batch else (lambda bi, i, j: (0, 0, 0))
    return pl.pallas_call(
        _norm_mm_body,
        grid=(b, l // tm, n // tn),
        in_specs=[
            pl.BlockSpec((1, tm, d), lambda bi, i, j: (bi, i, 0)),
            pl.BlockSpec((1, d), lambda bi, i, j: (0, 0)),
            pl.BlockSpec((1, 1, d), mod_idx),
            pl.BlockSpec((1, 1, d), mod_idx),
            pl.BlockSpec((d, tn), lambda bi, i, j: (0, j)),
        ],
        out_specs=pl.BlockSpec((1, nc, tm, V7X_LANES), lambda bi, i, j: (bi, j, i, 0)),
        out_shape=jax.ShapeDtypeStruct((b, n // V7X_LANES, l, V7X_LANES), BF16),
        scratch_shapes=[pltpu.VMEM((tm, d), BF16)],
        compiler_params=_cparams(("parallel", "parallel", "arbitrary"), 48),
        name="norm_modulate_project",
    )(x, g.reshape(1, d), sc, sh, w)


def _mm_resid_body(a0_ref, a1_ref, w0_ref, w1_ref, x_ref, gate_ref, o_ref):
    y = _dot(a0_ref[0], w0_ref[...]) + _dot(a1_ref[0], w1_ref[...])
    o_ref[0] = x_ref[0] + gate_ref[0] * y


def _mm_resid(a0, a1, w, x, gate):
    b, l, k0 = a0.shape
    k1 = a1.shape[2]
    n = w.shape[1]
    tm = _pick(l, 1024, V7X_SUBLANES)
    tn = _pick(n, 1024, 2 * V7X_LANES)
    per_batch = gate.shape[0] > 1
    gate_idx = (lambda bi, i, j: (bi, 0, j)) if per_batch else (lambda bi, i, j: (0, 0, j))
    return pl.pallas_call(
        _mm_resid_body,
        grid=(b, l // tm, n // tn),
        in_specs=[
            pl.BlockSpec((1, tm, k0), lambda bi, i, j: (bi, i, 0)),
            pl.BlockSpec((1, tm, k1), lambda bi, i, j: (bi, i, 0)),
            pl.BlockSpec((k0, tn), lambda bi, i, j: (0, j)),
            pl.BlockSpec((k1, tn), lambda bi, i, j: (k0 // k1, j)),
            pl.BlockSpec((1, tm, tn), lambda bi, i, j: (bi, i, j)),
            pl.BlockSpec((1, 1, tn), gate_idx),
        ],
        out_specs=pl.BlockSpec((1, tm, tn), lambda bi, i, j: (bi, i, j)),
        out_shape=jax.ShapeDtypeStruct((b, l, n), F32),
        compiler_params=_cparams(("parallel", "parallel", "arbitrary"), 48),
        name="project_gated_residual",
    )(a0, a1, w, w, x, gate)


def _router_logits(h32, h_hi, router):
    h_lo = (h32 - h_hi.astype(F32)).astype(BF16)
    r_hi = router.astype(BF16)
    r_lo = (router - r_hi.astype(F32)).astype(BF16)
    return _dot(h_hi, r_hi) + (_dot(h_lo, r_hi) + _dot(h_hi, r_lo))


def _top2(logits, n_experts):
    lane = lax.broadcasted_iota(jnp.int32, logits.shape, 1)
    lg = jnp.where(lane < n_experts, logits, NEG_BIG)
    m1 = jnp.max(lg, axis=-1, keepdims=True)
    i1 = jnp.min(jnp.where(lg == m1, lane, V7X_LANES), axis=-1, keepdims=True)
    lg2 = jnp.where(lane == i1, NEG_BIG, lg)
    m2 = jnp.max(lg2, axis=-1, keepdims=True)
    i2 = jnp.min(jnp.where(lg2 == m2, lane, V7X_LANES), axis=-1, keepdims=True)
    e2 = jnp.exp(m2 - m1)
    g1 = 1.0 / (1.0 + e2)
    return jnp.where(lane == 0, i1.astype(F32),
                     jnp.where(lane == 1, i2.astype(F32), jnp.where(lane == 2, g1, jnp.where(lane == 3, e2 * g1, 0.0))))


def _ffn_body(x_ref, g_ref, sc_ref, sh_ref, gate_ref, w1_ref, w3_ref, w2_ref, o_ref, h_ref, acc_ref):
    f = pl.program_id(2)

    @pl.when(f == 0)
    def _():
        _norm_modulate_rows(x_ref, g_ref[...], sc_ref[0], sh_ref[0], h_ref)
        acc_ref[...] = jnp.zeros_like(acc_ref)

    h = h_ref[...]
    act = _silu(_dot(h, w1_ref[...])) * _dot(h, w3_ref[...])
    acc_ref[...] += _dot(act.astype(BF16), w2_ref[...])

    @pl.when(f == pl.num_programs(2) - 1)
    def _():
        o_ref[0] = x_ref[0] + gate_ref[0] * acc_ref[...]


def _ffn(x, g, sc, sh, gate, w1, w3, w2):
    b, l, d = x.shape
    ff = w1.shape[1]
    tm = _pick(l, 512, V7X_SUBLANES)
    tf = _pick(ff, 512, V7X_LANES)
    per_batch = sc.shape[0] > 1
    mod_idx = (lambda bi, i, f: (bi, 0, 0)) if per_batch else (lambda bi, i, f: (0, 0, 0))
    return pl.pallas_call(
        _ffn_body,
        grid=(b, l // tm, ff // tf),
        in_specs=[
            pl.BlockSpec((1, tm, d), lambda bi, i, f: (bi, i, 0)),
            pl.BlockSpec((1, d), lambda bi, i, f: (0, 0)),
            pl.BlockSpec((1, 1, d), mod_idx),
            pl.BlockSpec((1, 1, d), mod_idx),
            pl.BlockSpec((1, 1, d), mod_idx),
            pl.BlockSpec((d, tf), lambda bi, i, f: (0, f)),
            pl.BlockSpec((d, tf), lambda bi, i, f: (0, f)),
            pl.BlockSpec((tf, d), lambda bi, i, f: (f, 0)),
        ],
        out_specs=pl.BlockSpec((1, tm, d), lambda bi, i, f: (bi, i, 0)),
        out_shape=jax.ShapeDtypeStruct((b, l, d), F32),
        scratch_shapes=[pltpu.VMEM((tm, d), BF16), pltpu.VMEM((tm, d), F32)],
        compiler_params=_cparams(("parallel", "parallel", "arbitrary"), 52),
        name="dense_swiglu",
    )(x, g.reshape(1, d), sc, sh, gate, w1, w3, w2)


MOE_TILE = 512
GATHER_ROWS = 256


def _route_body(x_ref, g_ref, sc_ref, sh_ref, r_ref, h_ref, info_ref, *, n_experts):
    h32 = _norm_modulate(x_ref[0], g_ref[...], sc_ref[0], sh_ref[0])
    h_ref[0] = h32
    info_ref[0] = _top2(_router_logits(h32, h32.astype(BF16), r_ref[...]), n_experts)


def _route(x, g, sc, sh, router, n_experts):
    b, l, d = x.shape
    tm = _pick(l, 512, V7X_SUBLANES)
    per_batch = sc.shape[0] > 1
    mod_idx = (lambda bi, i: (bi, 0, 0)) if per_batch else (lambda bi, i: (0, 0, 0))
    return pl.pallas_call(
        functools.partial(_route_body, n_experts=n_experts),
        grid=(b, l // tm),
        in_specs=[
            pl.BlockSpec((1, tm, d), lambda bi, i: (bi, i, 0)),
            pl.BlockSpec((1, d), lambda bi, i: (0, 0)),
            pl.BlockSpec((1, 1, d), mod_idx),
            pl.BlockSpec((1, 1, d), mod_idx),
            pl.BlockSpec((d, V7X_LANES), lambda bi, i: (0, 0)),
        ],
        out_specs=[pl.BlockSpec((1, tm, d), lambda bi, i: (bi, i, 0)),
                   pl.BlockSpec((1, tm, V7X_LANES), lambda bi, i: (bi, i, 0))],
        out_shape=[jax.ShapeDtypeStruct((b, l, d), F32), jax.ShapeDtypeStruct((b, l, V7X_LANES), F32)],
        compiler_params=_cparams(("parallel", "parallel"), 40),
        name="moe_route",
    )(x, g.reshape(1, d), sc, sh, router)


def _route_plan(info, n_experts, tm):
    n_tok = info.shape[0]
    ea = jnp.concatenate([info[:, 0], info[:, 1]]).astype(jnp.int32)
    onehot = (ea[:, None] == jnp.arange(n_experts, dtype=jnp.int32)[None, :]).astype(jnp.int32)
    rank = jnp.sum((jnp.cumsum(onehot, axis=0) - onehot) * onehot, axis=1)
    counts = jnp.sum(onehot, axis=0)
    padded = ((counts + tm - 1) // tm) * tm
    ends = jnp.cumsum(padded)
    slot = jnp.sum(onehot * (ends - padded)[None, :], axis=1) + rank
    n_rows = 2 * n_tok + (n_experts + 1) * tm
    n_tiles = n_rows // tm
    tok = jnp.tile(jnp.arange(n_tok, dtype=jnp.int32), 2)
    row_token = jnp.zeros((n_rows,), jnp.int32).at[slot].set(tok)
    starts = jnp.arange(n_tiles, dtype=jnp.int32) * tm
    tile_expert = jnp.minimum(jnp.sum((starts[:, None] >= ends[None, :]).astype(jnp.int32), axis=1), n_experts - 1)
    tile_valid = (starts < ends[-1]).astype(jnp.int32)
    return slot, row_token.reshape(n_tiles, tm), tile_expert, tile_valid


def _row_copy(src_ref, dst_ref, sem, src_row, dst_row):
    return pltpu.make_async_copy(src_ref.at[pl.ds(src_row, 1)], dst_ref.at[pl.ds(dst_row, 1)], sem)


def _experts_body(te_ref, tv_ref, idx0_ref, idxn_ref, h_ref, w1_ref, w3_ref, w2_ref, o_ref, hs_ref, hb_ref, sem, *, rps):
    i = pl.program_id(0)
    f = pl.program_id(1)
    tm = o_ref.shape[0]
    tp = hs_ref.shape[1]
    slot = i % 2
    valid = tv_ref[i] != 0
    prev_valid = (i > 0) & (tv_ref[jnp.maximum(i - 1, 0)] != 0)

    def fill_done(s):
        return pltpu.make_async_copy(h_ref.at[pl.ds(0, tp)], hs_ref.at[s], sem.at[s])

    @pl.when((i == 0) & (f == 0) & valid)
    def _():
        def issue(r, carry):
            _row_copy(h_ref, hs_ref.at[0], sem.at[0], idx0_ref[0, 0, r], r).start()
            return carry

        lax.fori_loop(0, tp, issue, 0, unroll=8)
        fill_done(0).wait()

    @pl.when((f == 0) & prev_valid)
    def _():
        fill_done(slot).wait()

    @pl.when(f == 0)
    def _():
        o_ref[...] = jnp.zeros_like(o_ref)

    @pl.when(valid & (f == 0))
    def _():
        hb_ref[...] = hs_ref[slot, 0:tm].astype(BF16)

    @pl.when(valid)
    def _():
        h = hb_ref[...]
        act = _silu(_dot(h, w1_ref[0])) * _dot(h, w3_ref[0])
        o_ref[...] += _dot(act.astype(BF16), w2_ref[0])
        base = f * rps
        for k in range(rps):
            _row_copy(h_ref, hs_ref.at[1 - slot], sem.at[1 - slot], idxn_ref[0, 0, base + k], base + k).start()


def _experts(h, row_token, tile_expert, tile_valid, w1, w3, w2):
    n_tiles, tm = row_token.shape
    d = h.shape[1]
    ff = w1.shape[2]
    tf = _pick(ff, 512, V7X_LANES)
    nf = ff // tf
    rps = -(-(-(-tm // nf)) // V7X_SUBLANES) * V7X_SUBLANES
    tp = rps * nf
    idx = jnp.pad(row_token, ((0, 0), (0, tp - tm))).reshape(n_tiles, 1, tp)
    f_eff = lambda i, f, tv: jnp.where(tv[i] != 0, f, nf - 1)
    grid_spec = pltpu.PrefetchScalarGridSpec(
        num_scalar_prefetch=2,
        grid=(n_tiles, nf),
        in_specs=[
            pl.BlockSpec((1, 1, tp), lambda i, f, te, tv: (0, 0, 0), memory_space=pltpu.SMEM),
            pl.BlockSpec((1, 1, tp), lambda i, f, te, tv: (jnp.minimum(i + 1, n_tiles - 1), 0, 0),
                         memory_space=pltpu.SMEM),
            pl.BlockSpec(memory_space=pl.ANY),
            pl.BlockSpec((1, d, tf), lambda i, f, te, tv: (te[i], 0, f_eff(i, f, tv))),
            pl.BlockSpec((1, d, tf), lambda i, f, te, tv: (te[i], 0, f_eff(i, f, tv))),
            pl.BlockSpec((1, tf, d), lambda i, f, te, tv: (te[i], f_eff(i, f, tv), 0)),
        ],
        out_specs=pl.BlockSpec((tm, d), lambda i, f, te, tv: (i, 0)),
        scratch_shapes=[pltpu.VMEM((2, tp, d), F32), pltpu.VMEM((tm, d), BF16), pltpu.SemaphoreType.DMA((2,))],
    )
    return pl.pallas_call(
        functools.partial(_experts_body, rps=rps),
        grid_spec=grid_spec,
        out_shape=jax.ShapeDtypeStruct((n_tiles * tm, d), F32),
        compiler_params=_cparams(("arbitrary", "arbitrary"), 48),
        name="routed_experts",
    )(tile_expert, tile_valid, idx, idx, h, w1, w3, w2)


def _combine_body(i0_ref, i1_ref, x_ref, gate_ref, info_ref, ys_ref, o_ref, buf_ref, sem, *, rows):
    def issue(r, carry):
        _row_copy(ys_ref, buf_ref.at[0], sem, i0_ref[0, 0, r], r).start()
        _row_copy(ys_ref, buf_ref.at[1], sem, i1_ref[0, 0, r], r).start()
        return carry

    lax.fori_loop(0, rows, issue, 0, unroll=8)
    for k in range(2):
        pltpu.make_async_copy(ys_ref.at[pl.ds(0, rows)], buf_ref.at[k], sem).wait()
    info = info_ref[...]
    y = info[:, 2:3] * buf_ref[0] + info[:, 3:4] * buf_ref[1]
    o_ref[...] = x_ref[...] + gate_ref[0] * y


def _combine(x, gate, info, ys, slot):
    b, l, d = x.shape
    n_tok = b * l
    rows = _pick(l, GATHER_ROWS, V7X_SUBLANES)
    per_batch = gate.shape[0] > 1
    blocks_per_seq = l // rows
    gate_idx = (lambda i: (i // blocks_per_seq, 0, 0)) if per_batch else (lambda i: (0, 0, 0))
    idx = slot.reshape(2, n_tok // rows, 1, rows)
    out = pl.pallas_call(
        functools.partial(_combine_body, rows=rows),
        grid=(n_tok // rows,),
        in_specs=[
            pl.BlockSpec((1, 1, rows), lambda i: (i, 0, 0), memory_space=pltpu.SMEM),
            pl.BlockSpec((1, 1, rows), lambda i: (i, 0, 0), memory_space=pltpu.SMEM),
            pl.BlockSpec((rows, d), lambda i: (i, 0)),
            pl.BlockSpec((1, 1, d), gate_idx),
            pl.BlockSpec((rows, V7X_LANES), lambda i: (i, 0)),
            pl.BlockSpec(memory_space=pl.ANY),
        ],
        out_specs=pl.BlockSpec((rows, d), lambda i: (i, 0)),
        out_shape=jax.ShapeDtypeStruct((n_tok, d), F32),
        scratch_shapes=[pltpu.VMEM((2, rows, d), F32), pltpu.SemaphoreType.DMA(())],
        compiler_params=_cparams(("arbitrary",), 32),
        name="moe_combine",
    )(idx[0], idx[1], x.reshape(n_tok, d), gate, info, ys)
    return out.reshape(b, l, d)


def _moe(x, g, sc, sh, gate, router, w1, w3, w2):
    b, l, d = x.shape
    n_experts = w1.shape[0]
    h, info = _route(x, g, sc, sh, router, n_experts)
    info = info.reshape(b * l, V7X_LANES)
    slot, row_token, tile_expert, tile_valid = _route_plan(info, n_experts, MOE_TILE)
    ys = _experts(h.reshape(b * l, d), row_token, tile_expert, tile_valid, w1, w3, w2)
    return _combine(x, gate, info, ys, slot)


HGRN_BLOCK = 128
HGRN_LEVEL_HALF_WIDTHS = (64, 32, 16, 8)
HGRN_DIAG = 8


@functools.lru_cache(maxsize=None)
def _hgrn_consts(reverse):
    n = HGRN_BLOCK
    t = np.arange(n)[:, None]
    s = np.arange(n)[None, :]
    before = (s >= t) if reverse else (s <= t)
    masks = []
    for w in HGRN_LEVEL_HALF_WIDTHS:
        same = (t // (2 * w)) == (s // (2 * w))
        t_late = (t % (2 * w)) >= w
        s_late = (s % (2 * w)) >= w
        masks.append(same & ~t_late & s_late if reverse else same & t_late & ~s_late)
    masks.append(((t // HGRN_DIAG) == (s // HGRN_DIAG)) & before)
    row_m = np.arange(HGRN_DIAG * n)[:, None] // n
    wsel = (np.arange(n)[None, :] % HGRN_DIAG) == row_m
    return (before.astype(np.float32), np.stack(masks).astype(np.float32), wsel.astype(np.float32))


def _group_row(x, n, r):
    x3 = x.reshape(HGRN_BLOCK // n, n, x.shape[-1])
    return jnp.broadcast_to(x3[:, r:r + 1, :], x3.shape).reshape(x.shape)


def _hgrn_heads(zqs, zfs, vs, lbs, sts, lc, mask_ref, wsel, reverse):
    n = HGRN_BLOCK
    heads = range(len(zqs))
    qs, kks, b3s = [], [], []
    for h in heads:
        sig = jax.nn.sigmoid(zfs[h].astype(F32))
        g = jnp.log2(lbs[h] + (1.0 - lbs[h]) * sig)
        kks.append((1.0 - lbs[h]) * (1.0 - sig))
        qs.append(_silu(zqs[h].astype(F32)))
        g_hi = g.astype(BF16)
        rem = g - g_hi.astype(F32)
        g_mid = rem.astype(BF16)
        g_lo = (rem - g_mid.astype(F32)).astype(BF16)
        b3s.append(_dot(lc, jnp.concatenate([g_hi, g_mid, g_lo], axis=1)))
    bs = [b3[:, :n] + (b3[:, n:2 * n] + b3[:, 2 * n:]) for b3 in b3s]
    accs = [None] * len(zqs)
    for li, w in enumerate(HGRN_LEVEL_HALF_WIDTHS):
        for h in heads:
            e = jnp.exp2(-jnp.abs(bs[h] - _group_row(bs[h], 2 * w, w if reverse else w - 1)))
            al = _dot_nt((qs[h] * e).astype(BF16), (kks[h] * e).astype(BF16)) * mask_ref[li]
            accs[h] = al if accs[h] is None else accs[h] + al
    for h in heads:
        vals = []
        for m in range(HGRN_DIAG):
            e = jnp.exp2(jnp.minimum(bs[h] - _group_row(bs[h], HGRN_DIAG, m), 0.0))
            vals.append((qs[h] * _group_row(kks[h], HGRN_DIAG, m) * e).astype(BF16))
        accs[h] = accs[h] + _dot(jnp.concatenate(vals, axis=1), wsel) * mask_ref[len(HGRN_LEVEL_HALF_WIDTHS)]
    end = 0 if reverse else n - 1
    outs = []
    for h in heads:
        b = bs[h]
        b_end = b[end:end + 1, :]
        o = _dot(accs[h].astype(BF16), vs[h]) + _dot_nt((qs[h] * jnp.exp2(b)).astype(BF16), sts[h].astype(BF16))
        k_end = (kks[h] * jnp.exp2(b_end - b)).astype(BF16)
        outs.append((o, sts[h] * jnp.exp2(b_end) + _dot_tn(vs[h], k_end)))
    return outs


HGRN_HEADS_PER_STEP = 4


def _hgrn_body(q_ref, f_ref, v_ref, lb_ref, s0_ref, lc_ref, mask_ref, wsel_ref, o_ref, st_ref, *, reverse, n_heads):
    @pl.when(pl.program_id(1) == 0)
    def _():
        st_ref[...] = s0_ref[...]

    k = HGRN_HEADS_PER_STEP if n_heads % HGRN_HEADS_PER_STEP == 0 else 1

    def group(gi, carry):
        hs = [gi * k + j for j in range(k)]
        outs = _hgrn_heads([q_ref[0, h] for h in hs], [f_ref[0, h] for h in hs], [v_ref[0, h] for h in hs],
                           [lb_ref[h] for h in hs], [st_ref[0, h] for h in hs],
                           lc_ref[...], mask_ref, wsel_ref[...], reverse)
        for h, (o, st_new) in zip(hs, outs):
            o_ref[0, h] = o.astype(o_ref.dtype)
            st_ref[0, h] = st_new
        return carry

    lax.fori_loop(0, n_heads // k, group, 0)


def _hgrn_scan(proj, lb, s0, n_heads, reverse):
    bsz, _, l, _ = proj.shape
    nb = l // HGRN_BLOCK
    lc, masks, wsel = _hgrn_consts(reverse)
    blk = (lambda i: nb - 1 - i) if reverse else (lambda i: i)
    f_seg = 2 if reverse else 1
    tile = (1, n_heads, HGRN_BLOCK, HEAD_DIM)
    full = lambda shape: pl.BlockSpec(shape, lambda b, i: (0,) * len(shape))
    return pl.pallas_call(
        functools.partial(_hgrn_body, reverse=reverse, n_heads=n_heads),
        grid=(bsz, nb),
        in_specs=[
            pl.BlockSpec(tile, lambda b, i: (b, 0, blk(i), 0)),
            pl.BlockSpec(tile, lambda b, i: (b, f_seg, blk(i), 0)),
            pl.BlockSpec(tile, lambda b, i: (b, 3, blk(i), 0)),
            full((n_heads, 1, HEAD_DIM)),
            pl.BlockSpec((1, n_heads, HEAD_DIM, HEAD_DIM), lambda b, i: (b, 0, 0, 0)),
            full((HGRN_BLOCK, HGRN_BLOCK)),
            full((len(HGRN_LEVEL_HALF_WIDTHS) + 1, HGRN_BLOCK, HGRN_BLOCK)),
            full((HGRN_DIAG * HGRN_BLOCK, HGRN_BLOCK)),
        ],
        out_specs=[
            pl.BlockSpec(tile, lambda b, i: (b, 0, blk(i), 0)),
            pl.BlockSpec((1, n_heads, HEAD_DIM, HEAD_DIM), lambda b, i: (b, 0, 0, 0)),
        ],
        out_shape=[
            jax.ShapeDtypeStruct((bsz, n_heads, l, HEAD_DIM), BF16),
            jax.ShapeDtypeStruct((bsz, n_heads, HEAD_DIM, HEAD_DIM), F32),
        ],
        compiler_params=_cparams(("parallel", "arbitrary"), 32),
        name="hgrn2_scan_bwd" if reverse else "hgrn2_scan_fwd",
    )(proj, proj, proj, lb, s0, jnp.asarray(lc, BF16), jnp.asarray(masks), jnp.asarray(wsel, BF16))


CONV_HALO = 16


def _conv_finish_body(of_ref, ob_ref, og_ref, a_ref, g_ref, ap_ref, gp_ref, an_ref, gn_ref, onorm_ref, w_ref,
                      cb_ref, lng_ref, lnb_ref, oa_ref, oc_ref, ext_ref, y_ref, *, n_heads, n_cc, k_taps):
    i = pl.program_id(1)
    last = pl.num_programs(1) - 1
    tt = a_ref.shape[2]
    lanes = V7X_LANES
    for h in range(n_heads):
        o = of_ref[0, h].astype(F32) + ob_ref[0, h].astype(F32)
        y = o * lax.rsqrt(jnp.mean(o * o, axis=-1, keepdims=True) + EPS) * onorm_ref[h]
        oa_ref[0, :, h * lanes:(h + 1) * lanes] = (y * _silu(og_ref[0, h].astype(F32))).astype(BF16)

    glu = lambda a, g: a.astype(F32) * jax.nn.sigmoid(g.astype(F32))
    first_tap = CONV_HALO - k_taps // 2
    tot = jnp.zeros((tt, lanes), F32)
    for c in range(n_cc):
        ext_ref[c, 0:CONV_HALO] = jnp.where(i > 0, glu(ap_ref[0, c], gp_ref[0, c]), 0.0)
        ext_ref[c, CONV_HALO:CONV_HALO + tt] = glu(a_ref[0, c], g_ref[0, c])
        ext_ref[c, CONV_HALO + tt:2 * CONV_HALO + tt] = jnp.where(i < last, glu(an_ref[0, c], gn_ref[0, c]), 0.0)
        acc = jnp.zeros((tt, lanes), F32)
        for j in range(k_taps):
            acc = acc + w_ref[c, j:j + 1, :] * ext_ref[c, first_tap + j:first_tap + j + tt, :]
        y = acc + cb_ref[c]
        y_ref[c] = y
        tot = tot + y
    inv_n = 1.0 / (n_cc * lanes)
    mu = jnp.sum(tot, axis=-1, keepdims=True) * inv_n
    sq = jnp.zeros((tt, lanes), F32)
    for c in range(n_cc):
        d = y_ref[c] - mu
        sq = sq + d * d
    rstd = lax.rsqrt(jnp.sum(sq, axis=-1, keepdims=True) * inv_n + EPS)
    for c in range(n_cc):
        un = (y_ref[c] - mu) * rstd * lng_ref[c] + lnb_ref[c]
        oc_ref[0, :, c * lanes:(c + 1) * lanes] = _silu(un).astype(BF16)


def _conv_finish(proj, o_f, o_b, onorm_g, dw_w, dw_b, ln_g, ln_b):
    bsz, _, l, lanes = proj.shape
    n_heads = o_f.shape[1]
    k_taps, bw = dw_w.shape
    n_cc = bw // lanes
    assert n_cc == n_heads and k_taps // 2 < CONV_HALO
    tt = _pick(l, 256, CONV_HALO)
    hb = tt // CONV_HALO
    n_halo = l // CONV_HALO
    seg_a = 5 * n_heads // n_cc
    tile = lambda seg: pl.BlockSpec((1, n_heads, tt, lanes), lambda b, i: (b, seg, i, 0))
    halo_prev = lambda seg: pl.BlockSpec((1, n_cc, CONV_HALO, lanes),
                                         lambda b, i: (b, seg, jnp.maximum(i * hb - 1, 0), 0))
    halo_next = lambda seg: pl.BlockSpec((1, n_cc, CONV_HALO, lanes),
                                         lambda b, i: (b, seg, jnp.minimum((i + 1) * hb, n_halo - 1), 0))
    full = lambda shape: pl.BlockSpec(shape, lambda b, i: (0,) * len(shape))
    per_chunk = lambda p: p.reshape(n_cc, 1, lanes)
    w = jnp.transpose(dw_w.reshape(k_taps, n_cc, lanes), (1, 0, 2))
    return pl.pallas_call(
        functools.partial(_conv_finish_body, n_heads=n_heads, n_cc=n_cc, k_taps=k_taps),
        grid=(bsz, l // tt),
        in_specs=[
            pl.BlockSpec((1, n_heads, tt, lanes), lambda b, i: (b, 0, i, 0)),
            pl.BlockSpec((1, n_heads, tt, lanes), lambda b, i: (b, 0, i, 0)),
            tile(4), tile(seg_a), tile(seg_a + 1),
            halo_prev(seg_a), halo_prev(seg_a + 1), halo_next(seg_a), halo_next(seg_a + 1),
            full((n_heads, 1, lanes)), full((n_cc, k_taps, lanes)),
            full((n_cc, 1, lanes)), full((n_cc, 1, lanes)), full((n_cc, 1, lanes)),
        ],
        out_specs=[
            pl.BlockSpec((1, tt, n_heads * lanes), lambda b, i: (b, i, 0)),
            pl.BlockSpec((1, tt, bw), lambda b, i: (b, i, 0)),
        ],
        out_shape=[
            jax.ShapeDtypeStruct((bsz, l, n_heads * lanes), BF16),
            jax.ShapeDtypeStruct((bsz, l, bw), BF16),
        ],
        scratch_shapes=[pltpu.VMEM((n_cc, tt + 2 * CONV_HALO, lanes), F32), pltpu.VMEM((n_cc, tt, lanes), F32)],
        compiler_params=_cparams(("parallel", "parallel"), 32),
        name="hgrn2_gate_conv_module",
    )(o_f, o_b, proj, proj, proj, proj, proj, proj, proj, onorm_g.reshape(n_heads, 1, lanes), w,
      per_chunk(dw_b), per_chunk(ln_g), per_chunk(ln_b))


def _rope_tables(n_tokens, head_dim, scale, rotate):
    lane = np.arange(V7X_LANES) % head_dim
    q4 = head_dim // 4
    col = (lane // (2 * q4)) * q4 + lane % q4
    sign = np.where(lane % (2 * q4) < q4, -1.0, 1.0).astype(np.float32)
    if not rotate:
        return (jnp.full((n_tokens, V7X_LANES), scale, F32), jnp.zeros((n_tokens, V7X_LANES), F32))
    t = jnp.arange(n_tokens)
    axis_dim = head_dim // 2
    inv_freq = ROPE_THETA ** (-jnp.arange(0, axis_dim, 2, dtype=F32) / axis_dim)
    ang = jnp.concatenate([(t // GRID_W).astype(F32)[:, None] * inv_freq,
                           (t % GRID_W).astype(F32)[:, None] * inv_freq], axis=-1)
    return (jnp.cos(ang)[:, col] * scale, jnp.sin(ang)[:, col] * (sign * scale))


def _qk_prep_one(x, gain, gmat, cos, sin, half_w):
    x = x.astype(F32)
    x2 = x * x
    hi = x2.astype(BF16)
    lo = (x2 - hi.astype(F32)).astype(BF16)
    y = x * lax.rsqrt(_dot(hi, gmat) + _dot(lo, gmat) + EPS) * gain
    lane = lax.broadcasted_iota(jnp.int32, y.shape, 1)
    partner = jnp.where(lane % (2 * half_w) < half_w,
                        pltpu.roll(y, V7X_LANES - half_w, 1), pltpu.roll(y, half_w, 1))
    return (y * cos + partner * sin).astype(BF16)


def _qk_prep_body(cq_ref, ck_ref, dq_ref, dk_ref, tab_ref, gain_ref, gmat_ref, o_ref, *, n_heads, n_kv):
    g64, g128 = gmat_ref[0], gmat_ref[1]
    d_half, g_half = HEAD_DIM // 8, HEAD_DIM // 4
    for h in range(n_heads):
        o_ref[0, h] = _qk_prep_one(cq_ref[0, h], gain_ref[0], g64, tab_ref[0], tab_ref[1], d_half)
        o_ref[0, n_heads + h] = _qk_prep_one(ck_ref[0, h], gain_ref[1], g64, tab_ref[2], tab_ref[3], d_half)
        o_ref[0, 2 * n_heads + h] = _qk_prep_one(dq_ref[0, h], gain_ref[2], g128, tab_ref[4], tab_ref[5], g_half)
    for h in range(n_kv):
        o_ref[0, 3 * n_heads + h] = _qk_prep_one(dk_ref[0, h], gain_ref[3], g128, tab_ref[6], tab_ref[7], g_half)


def _qk_prep(proj, gains, n_heads, n_kv, rotate):
    bsz, _, l, lanes = proj.shape
    diff_d = HEAD_DIM // 2
    log2e = math.log2(math.e)
    tabs = jnp.stack(_rope_tables(l, diff_d, diff_d ** -0.5 * log2e, rotate) + _rope_tables(l, diff_d, 1.0, rotate)
                     + _rope_tables(l, HEAD_DIM, HEAD_DIM ** -0.5 * log2e, rotate) + _rope_tables(l, HEAD_DIM, 1.0, rotate))
    lane = np.arange(lanes)
    gmat = np.stack([(lane[:, None] // diff_d == lane[None, :] // diff_d) / diff_d, np.full((lanes, lanes), 1.0 / lanes)])
    tt = _pick(l, 256, V7X_SUBLANES)
    tile = lambda n, idx: pl.BlockSpec((1, n, tt, lanes), lambda b, i: (b, idx, i, 0))
    n_out = 3 * n_heads + n_kv
    return pl.pallas_call(
        functools.partial(_qk_prep_body, n_heads=n_heads, n_kv=n_kv),
        grid=(bsz, l // tt),
        in_specs=[
            tile(n_heads, 0), tile(n_heads, 1), tile(n_heads, 3), tile(n_kv, 4 * n_heads // n_kv),
            pl.BlockSpec((8, tt, lanes), lambda b, i: (0, i, 0)),
            pl.BlockSpec((4, 1, lanes), lambda b, i: (0, 0, 0)),
            pl.BlockSpec((2, lanes, lanes), lambda b, i: (0, 0, 0)),
        ],
        out_specs=pl.BlockSpec((1, n_out, tt, lanes), lambda b, i: (b, 0, i, 0)),
        out_shape=jax.ShapeDtypeStruct((bsz, n_out, l, lanes), BF16),
        compiler_params=_cparams(("parallel", "parallel"), 32),
        name="qk_norm_rope",
    )(proj, proj, proj, proj, tabs, gains, jnp.asarray(gmat, BF16))


def _attn_body(lam_ref, subg_ref, q_ref, *refs, n_stack, diff, lam_init, chunks):
    n_src = len(chunks)
    kv_refs = refs[:2 * n_src]
    o_ref, qs_ref, m_ref, acc_ref = refs[2 * n_src:]
    tq = q_ref.shape[2]
    if diff:
        q = q_ref[0, 0]
        lane = lax.broadcasted_iota(jnp.int32, q.shape, 1)
        zero = jnp.zeros_like(q)
        qs_ref[0:tq] = jnp.where(lane < HEAD_DIM // 2, q, zero)
        qs_ref[tq:2 * tq] = jnp.where(lane >= HEAD_DIM // 2, q, zero)
    else:
        for g in range(n_stack):
            qs_ref[g * tq:(g + 1) * tq] = q_ref[0, g]
    m_ref[...] = jnp.full(m_ref.shape, NEG_BIG, F32)
    acc_ref[...] = jnp.zeros(acc_ref.shape, F32)
    lanes = V7X_LANES

    def visit(k, v):
        s = _dot_nt(qs_ref[...], k)
        n_t = s.shape[1] // lanes
        tiles = [s[:, t * lanes:(t + 1) * lanes] for t in range(n_t)]
        mt = tiles[0]
        for t in range(1, n_t):
            mt = jnp.maximum(mt, tiles[t])
        m_prev = m_ref[...]
        m_new = jnp.maximum(m_prev, jnp.max(mt, axis=-1, keepdims=True))
        alpha = jnp.exp2(m_prev - m_new)
        p = jnp.concatenate([jnp.exp2(t - m_new).astype(BF16) for t in tiles], axis=1)
        v_ext = jnp.concatenate([v, jnp.ones(v.shape, BF16)], axis=1)
        acc_ref[...] = jnp.concatenate([alpha, alpha], axis=1) * acc_ref[...] + _dot(p, v_ext)
        m_ref[...] = m_new

    for src, (tk, n_chunks) in enumerate(chunks):
        k_ref, v_ref = kv_refs[2 * src], kv_refs[2 * src + 1]
        if n_chunks == 1:
            visit(k_ref[0, 0], v_ref[0, 0])
        else:
            def step(ci, carry, k_ref=k_ref, v_ref=v_ref, tk=tk):
                rows = pl.ds(pl.multiple_of(ci * tk, tk), tk)
                visit(k_ref[0, 0, rows, :], v_ref[0, 0, rows, :])
                return carry
            lax.fori_loop(0, n_chunks, step, 0, unroll=8 if n_chunks % 8 == 0 else 1)

    o = acc_ref[:, 0:lanes] / acc_ref[:, lanes:2 * lanes]
    if diff:
        lp = lam_ref[...]
        lam = (jnp.exp(jnp.sum(lp[0:1] * lp[1:2], axis=-1, keepdims=True))
               - jnp.exp(jnp.sum(lp[2:3] * lp[3:4], axis=-1, keepdims=True)) + lam_init)
        d = o[0:tq] - lam * o[tq:2 * tq]
        y = d * lax.rsqrt(jnp.mean(d * d, axis=-1, keepdims=True) + EPS) * subg_ref[...]
        o_ref[0] = (y * (1.0 - lam_init)).astype(BF16)
    else:
        for g in range(n_stack):
            o_ref[0, :, g * HEAD_DIM:(g + 1) * HEAD_DIM] = o[g * tq:(g + 1) * tq].astype(BF16)


def _attention(qk_q, kv_srcs, lam_p, subln_g, lam_init, n_heads, n_kv, diff):
    bsz, _, lq, lanes = qk_q.shape
    group = n_heads // n_kv
    n_stack = 2 if diff else group
    n_units = n_heads if diff else n_kv
    tq = _pick(lq, 1024 // n_stack, V7X_SUBLANES)
    rows = n_stack * tq
    if diff:
        q_spec = pl.BlockSpec((1, 1, tq, lanes), lambda b, u, i: (b, u, i, 0))
        k_chunk = lambda u: n_heads + u
        v_chunk = lambda u: 2 * n_heads + u
        out_w = lanes
    else:
        q_spec = pl.BlockSpec((1, group, tq, lanes), lambda b, u, i: (b, 2 * n_heads // group + u, i, 0))
        k_chunk = lambda u: 3 * n_heads + u
        v_chunk = lambda u: 4 * n_heads + n_kv + u
        out_w = group * lanes
    in_specs = [pl.BlockSpec((4, lanes), lambda b, u, i: (0, 0)), pl.BlockSpec((1, lanes), lambda b, u, i: (0, 0)), q_spec]
    args = [lam_p, subln_g.reshape(1, lanes), qk_q]
    chunks = []
    for qk_k, proj in kv_srcs:
        lk = qk_k.shape[2]
        tk = _pick(lk, 512, V7X_SUBLANES)
        chunks.append((tk, lk // tk))
        in_specs.append(pl.BlockSpec((1, 1, lk, lanes), lambda b, u, i: (b, k_chunk(u), 0, 0)))
        in_specs.append(pl.BlockSpec((1, 1, lk, lanes), lambda b, u, i: (b, v_chunk(u), 0, 0)))
        args += [qk_k, proj]
    return pl.pallas_call(
        functools.partial(_attn_body, n_stack=n_stack, diff=diff, lam_init=lam_init, chunks=tuple(chunks)),
        grid=(bsz, n_units, lq // tq),
        in_specs=in_specs,
        out_specs=pl.BlockSpec((1, tq, out_w), lambda b, u, i: (b, i, u)),
        out_shape=jax.ShapeDtypeStruct((bsz, lq, n_heads * lanes), BF16),
        scratch_shapes=[pltpu.VMEM((rows, lanes), BF16), pltpu.VMEM((rows, lanes), F32),
                        pltpu.VMEM((rows, 2 * lanes), F32)],
        compiler_params=_cparams(("parallel", "parallel", "arbitrary"), 40),
        name="diff_attention" if diff else "gqa_attention",
    )(*args)


def kernel(x, c, ctx, c_ctx, ada_w, ada_b, norm_mix_g, norm_ffn_g, w_out, ab_w_in, hgrn_lb_logits, hgrn_onorm_g,
           conv_dw_w, conv_dw_b, conv_ln_g, conv_ln_b, cd_w_in, diff_qnorm_g, diff_knorm_g, diff_lambda, diff_subln_g,
           gqa_qnorm_g, gqa_knorm_g, ffn_w1, ffn_w3, ffn_w2, moe_router, moe_w1, moe_w3, moe_w2):
    bsz, _, d = x.shape
    depth = ada_w.shape[0]
    lanes = V7X_LANES
    n_heads = (d // 2) // HEAD_DIM
    n_kv = (cd_w_in.shape[2] - 4 * n_heads * HEAD_DIM) // (2 * HEAD_DIM)
    n_experts = moe_router.shape[2]

    rows = -(-(bsz + 1) // V7X_SUBLANES) * V7X_SUBLANES
    cond = jnp.zeros((rows, d), F32).at[:bsz].set(c).at[bsz].set(c_ctx)
    mods = _ada_all(cond, ada_w, ada_b)
    lbs = jnp.cumsum(jax.nn.softmax(hgrn_lb_logits.astype(F32), axis=0), axis=0)
    lbs = lbs - lbs[:1]

    for li in range(depth):
        j = li // 2
        need_ctx = li < depth - 1
        sh1, sc1, g1, sh2, sc2, g2 = [t.reshape(bsz, 1, d) for t in jnp.split(mods[li, :bsz], 6, axis=-1)]
        csh1, csc1, cg1, csh2, csc2, cg2 = [t.reshape(1, 1, d) for t in jnp.split(mods[li, bsz:bsz + 1], 6, axis=-1)]
        wo = _cast_bf16(w_out, li)
        if li % 2 == 0:
            w_in = _cast_bf16(ab_w_in, j)
            px = _norm_mm(x, norm_mix_g[li], sc1, sh1, w_in)
            pc = _norm_mm(ctx, norm_mix_g[li], csc1, csh1, w_in)
            lb_f = lbs[j, 0].reshape(n_heads, 1, HEAD_DIM)
            lb_b = lbs[j, 1].reshape(n_heads, 1, HEAD_DIM)
            s0 = jnp.zeros((bsz, n_heads, HEAD_DIM, HEAD_DIM), F32)
            ocf, s_f = _hgrn_scan(pc, lb_f, s0, n_heads, False)
            ocb, s_b = _hgrn_scan(pc, lb_b, s0, n_heads, True)
            oxf, _ = _hgrn_scan(px, lb_f, s_f, n_heads, False)
            oxb, _ = _hgrn_scan(px, lb_b, s_b, n_heads, True)
            conv_args = (hgrn_onorm_g[j], conv_dw_w[j], conv_dw_b[j], conv_ln_g[j], conv_ln_b[j])
            mix_x = _conv_finish(px, oxf, oxb, *conv_args)
            mix_c = _conv_finish(pc, ocf, ocb, *conv_args) if need_ctx else None
        else:
            w_in = _cast_bf16(cd_w_in, j)
            px = _norm_mm(x, norm_mix_g[li], sc1, sh1, w_in)
            pc = _norm_mm(ctx, norm_mix_g[li], csc1, csh1, w_in)
            tile2 = lambda g: jnp.tile(g, lanes // g.shape[0])
            gains = jnp.stack([tile2(diff_qnorm_g[j]), tile2(diff_knorm_g[j]), gqa_qnorm_g[j], gqa_knorm_g[j]])
            gains = gains.reshape(4, 1, lanes).astype(F32)
            qx = _qk_prep(px, gains, n_heads, n_kv, True)
            qc = _qk_prep(pc, gains, n_heads, n_kv, False)
            lam_init = 0.8 - 0.6 * math.exp(-0.3 * li)
            lam_p = jnp.zeros((4, lanes), F32).at[:, :diff_lambda.shape[2]].set(diff_lambda[j])
            attn = lambda q, srcs, diff: _attention(q, srcs, lam_p, diff_subln_g[j], lam_init, n_heads, n_kv, diff)
            srcs_x = [(qc, pc), (qx, px)]
            mix_x = (attn(qx, srcs_x, True), attn(qx, srcs_x, False))
            mix_c = (attn(qc, [(qc, pc)], True), attn(qc, [(qc, pc)], False)) if need_ctx else None
        x = _mm_resid(mix_x[0], mix_x[1], wo, x, g1)
        if need_ctx:
            ctx = _mm_resid(mix_c[0], mix_c[1], wo, ctx, cg1)

        if li % 2 == 0:
            mixer = functools.partial(_ffn, w1=_cast_bf16(ffn_w1, j), w3=_cast_bf16(ffn_w3, j), w2=_cast_bf16(ffn_w2, j))
        else:
            router = jnp.zeros((d, lanes), F32).at[:, :n_experts].set(moe_router[j])
            mixer = functools.partial(_moe, router=router, w1=_cast_bf16(moe_w1, j), w3=_cast_bf16(moe_w3, j),
                                      w2=_cast_bf16(moe_w2, j))
        x = mixer(x, norm_ffn_g[li], sc2, sh2, g2)
        if need_ctx:
            ctx = mixer(ctx, norm_ffn_g[li], csc2, csh2, cg2)
    return x
```

```python
import functools
import math

import numpy as np
import jax
import jax.numpy as jnp
from jax import lax
from jax.experimental import pallas as pl
from jax.experimental.pallas import tpu as pltpu

F32 = jnp.float32
BF16 = jnp.bfloat16
EPS = 1e-6
HEAD_DIM = 128
GRID_W = 64
ROPE_THETA = 10000.0
TOP_K = 2
NEG_BIG = -1e30
V7X_LANES = 128
V7X_SUBLANES = 8
MIB = 1 << 20


def _cparams(semantics, vmem_mib):
    return pltpu.CompilerParams(dimension_semantics=semantics, vmem_limit_bytes=int(vmem_mib * MIB))


def _pick(n, pref, mult):
    if n <= pref:
        return n
    t = (pref // mult) * mult
    while t >= mult:
        if n % t == 0:
            return t
        t -= mult
    return n


def _dot(a, b):
    return jnp.dot(a, b, preferred_element_type=F32)


def _dot_nt(a, b):
    return lax.dot_general(a, b, (((1,), (1,)), ((), ())), preferred_element_type=F32)


def _dot_tn(a, b):
    return lax.dot_general(a, b, (((0,), (0,)), ((), ())), preferred_element_type=F32)


def _silu(x):
    return x * jax.nn.sigmoid(x)


def _norm_modulate(x, g, sc, sh):
    ms = jnp.mean(x * x, axis=-1, keepdims=True)
    return (x * lax.rsqrt(ms + EPS)) * g * (1.0 + sc) + sh


NORM_ROWS = 16


def _norm_modulate_rows(x_ref, g, sc, sh, out_ref):
    gs = g * (1.0 + sc)

    def piece(r, carry):
        rows = pl.ds(pl.multiple_of(r * NORM_ROWS, NORM_ROWS), NORM_ROWS)
        x = x_ref[0, rows, :]
        ms = jnp.mean(x * x, axis=-1, keepdims=True)
        out_ref[rows, :] = (x * lax.rsqrt(ms + EPS) * gs + sh).astype(out_ref.dtype)
        return carry

    lax.fori_loop(0, out_ref.shape[0] // NORM_ROWS, piece, 0, unroll=8)


def _cast_body(w_ref, o_ref):
    o_ref[...] = w_ref[0].astype(BF16)


def _cast_bf16(w, lead):
    n, c = w.shape[0], w.shape[-1]
    r = math.prod(w.shape[1:-1])
    tr = _pick(r, 256, 2 * V7X_SUBLANES)
    out = pl.pallas_call(
        _cast_body,
        grid=(r // tr,),
        in_specs=[pl.BlockSpec((1, tr, c), lambda i: (lead, i, 0))],
        out_specs=pl.BlockSpec((tr, c), lambda i: (i, 0)),
        out_shape=jax.ShapeDtypeStruct((r, c), BF16),
        compiler_params=_cparams(("parallel",), 40),
        name="weights_to_bf16",
    )(w.reshape(n, r, c))
    return out.reshape(w.shape[1:])


def _ada_body(c_ref, w_ref, b_ref, o_ref):
    cond = c_ref[...]
    a = _silu(cond).astype(BF16)
    o_ref[0] = _dot(a, w_ref[0].astype(BF16)) + b_ref[0]


def _ada_all(cond, ada_w, ada_b):
    depth, d, n = ada_w.shape
    r = cond.shape[0]
    tn = _pick(n, 1024, V7X_LANES)
    return pl.pallas_call(
        _ada_body,
        grid=(depth, n // tn),
        in_specs=[
            pl.BlockSpec((r, d), lambda l, j: (0, 0)),
            pl.BlockSpec((1, d, tn), lambda l, j: (l, 0, j)),
            pl.BlockSpec((1, 1, tn), lambda l, j: (l, 0, j)),
        ],
        out_specs=pl.BlockSpec((1, r, tn), lambda l, j: (l, 0, j)),
        out_shape=jax.ShapeDtypeStruct((depth, r, n), F32),
        compiler_params=_cparams(("arbitrary", "arbitrary"), 40),
        name="ada_modulation",
    )(cond, ada_w, ada_b.reshape(depth, 1, n))


def _norm_mm_body(x_ref, g_ref, sc_ref, sh_ref, w_ref, o_ref, h_ref):
    @pl.when(pl.program_id(2) == 0)
    def _():
        _norm_modulate_rows(x_ref, g_ref[...], sc_ref[0], sh_ref[0], h_ref)

    res = _dot(h_ref[...], w_ref[...]).astype(o_ref.dtype)
    for c in range(o_ref.shape[1]):
        o_ref[0, c] = res[:, c * V7X_LANES:(c + 1) * V7X_LANES]


def _norm_mm(x, g, sc, sh, w):
    b, l, d = x.shape
    n = w.shape[1]
    tm = _pick(l, 1024, V7X_SUBLANES)
    tn = _pick(n, 1024, 2 * V7X_LANES)
    nc = tn // V7X_LANES
    per_batch = sc.shape[0] > 1
    mod_idx = (lambda bi, i, j: (bi, 0, 0)) if per_batch else (lambda bi, i, j: (0, 0, 0))
    return pl.pallas_call(
        _norm_mm_body,
        grid=(b, l // tm, n // tn),
        in_specs=[
            pl.BlockSpec((1, tm, d), lambda bi, i, j: (bi, i, 0)),
            pl.BlockSpec((1, d), lambda bi, i, j: (0, 0)),
            pl.BlockSpec((1, 1, d), mod_idx),
            pl.BlockSpec((1, 1, d), mod_idx),
            pl.BlockSpec((d, tn), lambda bi, i, j: (0, j)),
        ],
        out_specs=pl.BlockSpec((1, nc, tm, V7X_LANES), lambda bi, i, j: (bi, j, i, 0)),
        out_shape=jax.ShapeDtypeStruct((b, n // V7X_LANES, l, V7X_LANES), BF16),
        scratch_shapes=[pltpu.VMEM((tm, d), BF16)],
        compiler_params=_cparams(("parallel", "parallel", "arbitrary"), 48),
        name="norm_modulate_project",
    )(x, g.reshape(1, d), sc, sh, w)


def _mm_resid_body(a0_ref, a1_ref, w0_ref, w1_ref, x_ref, gate_ref, o_ref):
    y = _dot(a0_ref[0], w0_ref[...]) + _dot(a1_ref[0], w1_ref[...])
    o_ref[0] = x_ref[0] + gate_ref[0] * y


def _mm_resid(a0, a1, w, x, gate):
    b, l, k0 = a0.shape
    k1 = a1.shape[2]
    n = w.shape[1]
    tm = _pick(l, 1024, V7X_SUBLANES)
    tn = _pick(n, 1024, 2 * V7X_LANES)
    per_batch = gate.shape[0] > 1
    gate_idx = (lambda bi, i, j: (bi, 0, j)) if per_batch else (lambda bi, i, j: (0, 0, j))
    return pl.pallas_call(
        _mm_resid_body,
        grid=(b, l // tm, n // tn),
        in_specs=[
            pl.BlockSpec((1, tm, k0), lambda bi, i, j: (bi, i, 0)),
            pl.BlockSpec((1, tm, k1), lambda bi, i, j: (bi, i, 0)),
            pl.BlockSpec((k0, tn), lambda bi, i, j: (0, j)),
            pl.BlockSpec((k1, tn), lambda bi, i, j: (k0 // k1, j)),
            pl.BlockSpec((1, tm, tn), lambda bi, i, j: (bi, i, j)),
            pl.BlockSpec((1, 1, tn), gate_idx),
        ],
        out_specs=pl.BlockSpec((1, tm, tn), lambda bi, i, j: (bi, i, j)),
        out_shape=jax.ShapeDtypeStruct((b, l, n), F32),
        compiler_params=_cparams(("parallel", "parallel", "arbitrary"), 48),
        name="project_gated_residual",
    )(a0, a1, w, w, x, gate)


def _router_logits(h32, h_hi, router):
    h_lo = (h32 - h_hi.astype(F32)).astype(BF16)
    r_hi = router.astype(BF16)
    r_lo = (router - r_hi.astype(F32)).astype(BF16)
    return _dot(h_hi, r_hi) + (_dot(h_lo, r_hi) + _dot(h_hi, r_lo))


def _top2(logits, n_experts):
    lane = lax.broadcasted_iota(jnp.int32, logits.shape, 1)
    lg = jnp.where(lane < n_experts, logits, NEG_BIG)
    m1 = jnp.max(lg, axis=-1, keepdims=True)
    i1 = jnp.min(jnp.where(lg == m1, lane, V7X_LANES), axis=-1, keepdims=True)
    lg2 = jnp.where(lane == i1, NEG_BIG, lg)
    m2 = jnp.max(lg2, axis=-1, keepdims=True)
    i2 = jnp.min(jnp.where(lg2 == m2, lane, V7X_LANES), axis=-1, keepdims=True)
    e2 = jnp.exp(m2 - m1)
    g1 = 1.0 / (1.0 + e2)
    return jnp.where(lane == 0, i1.astype(F32),
                     jnp.where(lane == 1, i2.astype(F32), jnp.where(lane == 2, g1, jnp.where(lane == 3, e2 * g1, 0.0))))


def _ffn_body(x_ref, g_ref, sc_ref, sh_ref, gate_ref, w1_ref, w3_ref, w2_ref, o_ref, h_ref, acc_ref):
    f = pl.program_id(2)

    @pl.when(f == 0)
    def _():
        _norm_modulate_rows(x_ref, g_ref[...], sc_ref[0], sh_ref[0], h_ref)
        acc_ref[...] = jnp.zeros_like(acc_ref)

    h = h_ref[...]
    act = _silu(_dot(h, w1_ref[...])) * _dot(h, w3_ref[...])
    acc_ref[...] += _dot(act.astype(BF16), w2_ref[...])

    @pl.when(f == pl.num_programs(2) - 1)
    def _():
        o_ref[0] = x_ref[0] + gate_ref[0] * acc_ref[...]


def _ffn(x, g, sc, sh, gate, w1, w3, w2):
    b, l, d = x.shape
    ff = w1.shape[1]
    tm = _pick(l, 512, V7X_SUBLANES)
    tf = _pick(ff, 512, V7X_LANES)
    per_batch = sc.shape[0] > 1
    mod_idx = (lambda bi, i, f: (bi, 0, 0)) if per_batch else (lambda bi, i, f: (0, 0, 0))
    return pl.pallas_call(
        _ffn_body,
        grid=(b, l // tm, ff // tf),
        in_specs=[
            pl.BlockSpec((1, tm, d), lambda bi, i, f: (bi, i, 0)),
            pl.BlockSpec((1, d), lambda bi, i, f: (0, 0)),
            pl.BlockSpec((1, 1, d), mod_idx),
            pl.BlockSpec((1, 1, d), mod_idx),
            pl.BlockSpec((1, 1, d), mod_idx),
            pl.BlockSpec((d, tf), lambda bi, i, f: (0, f)),
            pl.BlockSpec((d, tf), lambda bi, i, f: (0, f)),
            pl.BlockSpec((tf, d), lambda bi, i, f: (f, 0)),
        ],
        out_specs=pl.BlockSpec((1, tm, d), lambda bi, i, f: (bi, i, 0)),
        out_shape=jax.ShapeDtypeStruct((b, l, d), F32),
        scratch_shapes=[pltpu.VMEM((tm, d), BF16), pltpu.VMEM((tm, d), F32)],
        compiler_params=_cparams(("parallel", "parallel", "arbitrary"), 52),
        name="dense_swiglu",
    )(x, g.reshape(1, d), sc, sh, gate, w1, w3, w2)


MOE_TILE = 512
GATHER_ROWS = 256


def _route_body(x_ref, g_ref, sc_ref, sh_ref, r_ref, h_ref, info_ref, *, n_experts):
    h32 = _norm_modulate(x_ref[0], g_ref[...], sc_ref[0], sh_ref[0])
    h_ref[0] = h32
    info_ref[0] = _top2(_router_logits(h32, h32.astype(BF16), r_ref[...]), n_experts)


def _route(x, g, sc, sh, router, n_experts):
    b, l, d = x.shape
    tm = _pick(l, 512, V7X_SUBLANES)
    per_batch = sc.shape[0] > 1
    mod_idx = (lambda bi, i: (bi, 0, 0)) if per_batch else (lambda bi, i: (0, 0, 0))
    return pl.pallas_call(
        functools.partial(_route_body, n_experts=n_experts),
        grid=(b, l // tm),
        in_specs=[
            pl.BlockSpec((1, tm, d), lambda bi, i: (bi, i, 0)),
            pl.BlockSpec((1, d), lambda bi, i: (0, 0)),
            pl.BlockSpec((1, 1, d), mod_idx),
            pl.BlockSpec((1, 1, d), mod_idx),
            pl.BlockSpec((d, V7X_LANES), lambda bi, i: (0, 0)),
        ],
        out_specs=[pl.BlockSpec((1, tm, d), lambda bi, i: (bi, i, 0)),
                   pl.BlockSpec((1, tm, V7X_LANES), lambda bi, i: (bi, i, 0))],
        out_shape=[jax.ShapeDtypeStruct((b, l, d), F32), jax.ShapeDtypeStruct((b, l, V7X_LANES), F32)],
        compiler_params=_cparams(("parallel", "parallel"), 40),
        name="moe_route",
    )(x, g.reshape(1, d), sc, sh, router)


def _route_plan(info, n_experts, tm):
    n_tok = info.shape[0]
    ea = jnp.concatenate([info[:, 0], info[:, 1]]).astype(jnp.int32)
    onehot = (ea[:, None] == jnp.arange(n_experts, dtype=jnp.int32)[None, :]).astype(jnp.int32)
    rank = jnp.sum((jnp.cumsum(onehot, axis=0) - onehot) * onehot, axis=1)
    counts = jnp.sum(onehot, axis=0)
    padded = ((counts + tm - 1) // tm) * tm
    ends = jnp.cumsum(padded)
    slot = jnp.sum(onehot * (ends - padded)[None, :], axis=1) + rank
    n_rows = 2 * n_tok + (n_experts + 1) * tm
    n_tiles = n_rows // tm
    tok = jnp.tile(jnp.arange(n_tok, dtype=jnp.int32), 2)
    row_token = jnp.zeros((n_rows,), jnp.int32).at[slot].set(tok)
    starts = jnp.arange(n_tiles, dtype=jnp.int32) * tm
    tile_expert = jnp.minimum(jnp.sum((starts[:, None] >= ends[None, :]).astype(jnp.int32), axis=1), n_experts - 1)
    tile_valid = (starts < ends[-1]).astype(jnp.int32)
    return slot, row_token.reshape(n_tiles, tm), tile_expert, tile_valid


def _row_copy(src_ref, dst_ref, sem, src_row, dst_row):
    return pltpu.make_async_copy(src_ref.at[pl.ds(src_row, 1)], dst_ref.at[pl.ds(dst_row, 1)], sem)


def _experts_body(te_ref, tv_ref, idx0_ref, idxn_ref, h_ref, w1_ref, w3_ref, w2_ref, o_ref, hs_ref, hb_ref, sem, *, rps):
    i = pl.program_id(0)
    f = pl.program_id(1)
    tm = o_ref.shape[0]
    tp = hs_ref.shape[1]
    slot = i % 2
    valid = tv_ref[i] != 0
    prev_valid = (i > 0) & (tv_ref[jnp.maximum(i - 1, 0)] != 0)

    def fill_done(s):
        return pltpu.make_async_copy(h_ref.at[pl.ds(0, tp)], hs_ref.at[s], sem.at[s])

    @pl.when((i == 0) & (f == 0) & valid)
    def _():
        def issue(r, carry):
            _row_copy(h_ref, hs_ref.at[0], sem.at[0], idx0_ref[0, 0, r], r).start()
            return carry

        lax.fori_loop(0, tp, issue, 0, unroll=8)
        fill_done(0).wait()

    @pl.when((f == 0) & prev_valid)
    def _():
        fill_done(slot).wait()

    @pl.when(f == 0)
    def _():
        o_ref[...] = jnp.zeros_like(o_ref)

    @pl.when(valid & (f == 0))
    def _():
        hb_ref[...] = hs_ref[slot, 0:tm].astype(BF16)

    @pl.when(valid)
    def _():
        h = hb_ref[...]
        act = _silu(_dot(h, w1_ref[0])) * _dot(h, w3_ref[0])
        o_ref[...] += _dot(act.astype(BF16), w2_ref[0])
        base = f * rps
        for k in range(rps):
            _row_copy(h_ref, hs_ref.at[1 - slot], sem.at[1 - slot], idxn_ref[0, 0, base + k],
                      base + k).start(priority=1)


def _experts(h, row_token, tile_expert, tile_valid, w1, w3, w2):
    n_tiles, tm = row_token.shape
    d = h.shape[1]
    ff = w1.shape[2]
    tf = _pick(ff, 512, V7X_LANES)
    nf = ff // tf
    rps = -(-(-(-tm // nf)) // V7X_SUBLANES) * V7X_SUBLANES
    tp = rps * nf
    idx = jnp.pad(row_token, ((0, 0), (0, tp - tm))).reshape(n_tiles, 1, tp)
    f_eff = lambda i, f, tv: jnp.where(tv[i] != 0, f, nf - 1)
    grid_spec = pltpu.PrefetchScalarGridSpec(
        num_scalar_prefetch=2,
        grid=(n_tiles, nf),
        in_specs=[
            pl.BlockSpec((1, 1, tp), lambda i, f, te, tv: (0, 0, 0), memory_space=pltpu.SMEM),
            pl.BlockSpec((1, 1, tp), lambda i, f, te, tv: (jnp.minimum(i + 1, n_tiles - 1), 0, 0),
                         memory_space=pltpu.SMEM),
            pl.BlockSpec(memory_space=pl.ANY),
            pl.BlockSpec((1, d, tf), lambda i, f, te, tv: (te[i], 0, f_eff(i, f, tv))),
            pl.BlockSpec((1, d, tf), lambda i, f, te, tv: (te[i], 0, f_eff(i, f, tv))),
            pl.BlockSpec((1, tf, d), lambda i, f, te, tv: (te[i], f_eff(i, f, tv), 0)),
        ],
        out_specs=pl.BlockSpec((tm, d), lambda i, f, te, tv: (i, 0)),
        scratch_shapes=[pltpu.VMEM((2, tp, d), F32), pltpu.VMEM((tm, d), BF16), pltpu.SemaphoreType.DMA((2,))],
    )
    return pl.pallas_call(
        functools.partial(_experts_body, rps=rps),
        grid_spec=grid_spec,
        out_shape=jax.ShapeDtypeStruct((n_tiles * tm, d), F32),
        compiler_params=_cparams(("arbitrary", "arbitrary"), 48),
        name="routed_experts",
    )(tile_expert, tile_valid, idx, idx, h, w1, w3, w2)


def _combine_body(i0_ref, i1_ref, n0_ref, n1_ref, x_ref, gate_ref, info_ref, ys_ref, o_ref, buf_ref, sem, *, rows):
    i = pl.program_id(0)
    slot = i % 2

    def fill(a_ref, b_ref, s):
        def issue(r, carry):
            _row_copy(ys_ref, buf_ref.at[s, 0], sem.at[s, 0], a_ref[0, 0, r], r).start()
            _row_copy(ys_ref, buf_ref.at[s, 1], sem.at[s, 1], b_ref[0, 0, r], r).start(priority=1)
            return carry

        lax.fori_loop(0, rows, issue, 0, unroll=8)

    @pl.when(i == 0)
    def _():
        fill(i0_ref, i1_ref, 0)

    @pl.when(i + 1 < pl.num_programs(0))
    def _():
        fill(n0_ref, n1_ref, 1 - slot)

    for k in range(2):
        pltpu.make_async_copy(ys_ref.at[pl.ds(0, rows)], buf_ref.at[slot, k], sem.at[slot, k]).wait()
    info = info_ref[...]
    y = info[:, 2:3] * buf_ref[slot, 0] + info[:, 3:4] * buf_ref[slot, 1]
    o_ref[...] = x_ref[...] + gate_ref[0] * y


def _combine(x, gate, info, ys, slot):
    b, l, d = x.shape
    n_tok = b * l
    rows = _pick(l, GATHER_ROWS, V7X_SUBLANES)
    per_batch = gate.shape[0] > 1
    blocks_per_seq = l // rows
    gate_idx = (lambda i: (i // blocks_per_seq, 0, 0)) if per_batch else (lambda i: (0, 0, 0))
    n_steps = n_tok // rows
    idx = slot.reshape(2, n_steps, 1, rows)
    this_step = pl.BlockSpec((1, 1, rows), lambda i: (i, 0, 0), memory_space=pltpu.SMEM)
    next_step = pl.BlockSpec((1, 1, rows), lambda i: (jnp.minimum(i + 1, n_steps - 1), 0, 0), memory_space=pltpu.SMEM)
    out = pl.pallas_call(
        functools.partial(_combine_body, rows=rows),
        grid=(n_steps,),
        in_specs=[
            this_step, this_step, next_step, next_step,
            pl.BlockSpec((rows, d), lambda i: (i, 0)),
            pl.BlockSpec((1, 1, d), gate_idx),
            pl.BlockSpec((rows, V7X_LANES), lambda i: (i, 0)),
            pl.BlockSpec(memory_space=pl.ANY),
        ],
        out_specs=pl.BlockSpec((rows, d), lambda i: (i, 0)),
        out_shape=jax.ShapeDtypeStruct((n_tok, d), F32),
        scratch_shapes=[pltpu.VMEM((2, 2, rows, d), F32), pltpu.SemaphoreType.DMA((2, 2))],
        compiler_params=_cparams(("arbitrary",), 40),
        name="moe_combine",
    )(idx[0], idx[1], idx[0], idx[1], x.reshape(n_tok, d), gate, info, ys)
    return out.reshape(b, l, d)


def _moe(x, g, sc, sh, gate, router, w1, w3, w2):
    b, l, d = x.shape
    n_experts = w1.shape[0]
    h, info = _route(x, g, sc, sh, router, n_experts)
    info = info.reshape(b * l, V7X_LANES)
    slot, row_token, tile_expert, tile_valid = _route_plan(info, n_experts, MOE_TILE)
    ys = _experts(h.reshape(b * l, d), row_token, tile_expert, tile_valid, w1, w3, w2)
    return _combine(x, gate, info, ys, slot)


HGRN_BLOCK = 128
HGRN_LEVEL_HALF_WIDTHS = (64, 32, 16, 8)
HGRN_DIAG = 8


@functools.lru_cache(maxsize=None)
def _hgrn_consts(reverse):
    n = HGRN_BLOCK
    t = np.arange(n)[:, None]
    s = np.arange(n)[None, :]
    before = (s >= t) if reverse else (s <= t)
    masks = []
    for w in HGRN_LEVEL_HALF_WIDTHS:
        same = (t // (2 * w)) == (s // (2 * w))
        t_late = (t % (2 * w)) >= w
        s_late = (s % (2 * w)) >= w
        masks.append(same & ~t_late & s_late if reverse else same & t_late & ~s_late)
    masks.append(((t // HGRN_DIAG) == (s // HGRN_DIAG)) & before)
    row_m = np.arange(HGRN_DIAG * n)[:, None] // n
    wsel = (np.arange(n)[None, :] % HGRN_DIAG) == row_m
    return (before.astype(np.float32), np.stack(masks).astype(np.float32), wsel.astype(np.float32))


def _group_row(x, n, r):
    x3 = x.reshape(HGRN_BLOCK // n, n, x.shape[-1])
    return jnp.broadcast_to(x3[:, r:r + 1, :], x3.shape).reshape(x.shape)


def _hgrn_heads(zqs, zfs, vs, lbs, sts, lc, mask_ref, wsel, reverse):
    n = HGRN_BLOCK
    heads = range(len(zqs))
    qs, kks, b3s = [], [], []
    for h in heads:
        sig = jax.nn.sigmoid(zfs[h].astype(F32))
        g = jnp.log2(lbs[h] + (1.0 - lbs[h]) * sig)
        kks.append((1.0 - lbs[h]) * (1.0 - sig))
        qs.append(_silu(zqs[h].astype(F32)))
        g_hi = g.astype(BF16)
        rem = g - g_hi.astype(F32)
        g_mid = rem.astype(BF16)
        g_lo = (rem - g_mid.astype(F32)).astype(BF16)
        b3s.append(_dot(lc, jnp.concatenate([g_hi, g_mid, g_lo], axis=1)))
    bs = [b3[:, :n] + (b3[:, n:2 * n] + b3[:, 2 * n:]) for b3 in b3s]
    accs = [None] * len(zqs)
    for li, w in enumerate(HGRN_LEVEL_HALF_WIDTHS):
        for h in heads:
            e = jnp.exp2(-jnp.abs(bs[h] - _group_row(bs[h], 2 * w, w if reverse else w - 1)))
            al = _dot_nt((qs[h] * e).astype(BF16), (kks[h] * e).astype(BF16)) * mask_ref[li]
            accs[h] = al if accs[h] is None else accs[h] + al
    for h in heads:
        vals = []
        for m in range(HGRN_DIAG):
            e = jnp.exp2(jnp.minimum(bs[h] - _group_row(bs[h], HGRN_DIAG, m), 0.0))
            vals.append((qs[h] * _group_row(kks[h], HGRN_DIAG, m) * e).astype(BF16))
        accs[h] = accs[h] + _dot(jnp.concatenate(vals, axis=1), wsel) * mask_ref[len(HGRN_LEVEL_HALF_WIDTHS)]
    end = 0 if reverse else n - 1
    outs = []
    for h in heads:
        b = bs[h]
        b_end = b[end:end + 1, :]
        o = _dot(accs[h].astype(BF16), vs[h]) + _dot_nt((qs[h] * jnp.exp2(b)).astype(BF16), sts[h].astype(BF16))
        k_end = (kks[h] * jnp.exp2(b_end - b)).astype(BF16)
        outs.append((o, sts[h] * jnp.exp2(b_end) + _dot_tn(vs[h], k_end)))
    return outs


HGRN_HEADS_PER_STEP = 4


def _hgrn_body(q_ref, f_ref, v_ref, lb_ref, s0_ref, lc_ref, mask_ref, wsel_ref, o_ref, st_ref, *, reverse, n_heads):
    @pl.when(pl.program_id(1) == 0)
    def _():
        st_ref[...] = s0_ref[...]

    k = HGRN_HEADS_PER_STEP if n_heads % HGRN_HEADS_PER_STEP == 0 else 1

    def group(gi, carry):
        hs = [gi * k + j for j in range(k)]
        outs = _hgrn_heads([q_ref[0, h] for h in hs], [f_ref[0, h] for h in hs], [v_ref[0, h] for h in hs],
                           [lb_ref[h] for h in hs], [st_ref[0, h] for h in hs],
                           lc_ref[...], mask_ref, wsel_ref[...], reverse)
        for h, (o, st_new) in zip(hs, outs):
            o_ref[0, h] = o.astype(o_ref.dtype)
            st_ref[0, h] = st_new
        return carry

    lax.fori_loop(0, n_heads // k, group, 0)


def _hgrn_scan(proj, lb, s0, n_heads, reverse):
    bsz, _, l, _ = proj.shape
    nb = l // HGRN_BLOCK
    lc, masks, wsel = _hgrn_consts(reverse)
    blk = (lambda i: nb - 1 - i) if reverse else (lambda i: i)
    f_seg = 2 if reverse else 1
    tile = (1, n_heads, HGRN_BLOCK, HEAD_DIM)
    full = lambda shape: pl.BlockSpec(shape, lambda b, i: (0,) * len(shape))
    return pl.pallas_call(
        functools.partial(_hgrn_body, reverse=reverse, n_heads=n_heads),
        grid=(bsz, nb),
        in_specs=[
            pl.BlockSpec(tile, lambda b, i: (b, 0, blk(i), 0)),
            pl.BlockSpec(tile, lambda b, i: (b, f_seg, blk(i), 0)),
            pl.BlockSpec(tile, lambda b, i: (b, 3, blk(i), 0)),
            full((n_heads, 1, HEAD_DIM)),
            pl.BlockSpec((1, n_heads, HEAD_DIM, HEAD_DIM), lambda b, i: (b, 0, 0, 0)),
            full((HGRN_BLOCK, HGRN_BLOCK)),
            full((len(HGRN_LEVEL_HALF_WIDTHS) + 1, HGRN_BLOCK, HGRN_BLOCK)),
            full((HGRN_DIAG * HGRN_BLOCK, HGRN_BLOCK)),
        ],
        out_specs=[
            pl.BlockSpec(tile, lambda b, i: (b, 0, blk(i), 0)),
            pl.BlockSpec((1, n_heads, HEAD_DIM, HEAD_DIM), lambda b, i: (b, 0, 0, 0)),
        ],
        out_shape=[
            jax.ShapeDtypeStruct((bsz, n_heads, l, HEAD_DIM), BF16),
            jax.ShapeDtypeStruct((bsz, n_heads, HEAD_DIM, HEAD_DIM), F32),
        ],
        compiler_params=_cparams(("parallel", "arbitrary"), 32),
        name="hgrn2_scan_bwd" if reverse else "hgrn2_scan_fwd",
    )(proj, proj, proj, lb, s0, jnp.asarray(lc, BF16), jnp.asarray(masks), jnp.asarray(wsel, BF16))


CONV_HALO = 16


def _conv_finish_body(of_ref, ob_ref, og_ref, a_ref, g_ref, ap_ref, gp_ref, an_ref, gn_ref, onorm_ref, w_ref,
                      cb_ref, lng_ref, lnb_ref, oa_ref, oc_ref, ext_ref, y_ref, *, n_heads, n_cc, k_taps):
    i = pl.program_id(1)
    last = pl.num_programs(1) - 1
    tt = a_ref.shape[2]
    lanes = V7X_LANES
    for h in range(n_heads):
        o = of_ref[0, h].astype(F32) + ob_ref[0, h].astype(F32)
        y = o * lax.rsqrt(jnp.mean(o * o, axis=-1, keepdims=True) + EPS) * onorm_ref[h]
        oa_ref[0, :, h * lanes:(h + 1) * lanes] = (y * _silu(og_ref[0, h].astype(F32))).astype(BF16)

    glu = lambda a, g: a.astype(F32) * jax.nn.sigmoid(g.astype(F32))
    first_tap = CONV_HALO - k_taps // 2
    tot = jnp.zeros((tt, lanes), F32)
    for c in range(n_cc):
        ext_ref[c, 0:CONV_HALO] = jnp.where(i > 0, glu(ap_ref[0, c], gp_ref[0, c]), 0.0)
        ext_ref[c, CONV_HALO:CONV_HALO + tt] = glu(a_ref[0, c], g_ref[0, c])
        ext_ref[c, CONV_HALO + tt:2 * CONV_HALO + tt] = jnp.where(i < last, glu(an_ref[0, c], gn_ref[0, c]), 0.0)
        acc = jnp.zeros((tt, lanes), F32)
        for j in range(k_taps):
            acc = acc + w_ref[c, j:j + 1, :] * ext_ref[c, first_tap + j:first_tap + j + tt, :]
        y = acc + cb_ref[c]
        y_ref[c] = y
        tot = tot + y
    inv_n = 1.0 / (n_cc * lanes)
    mu = jnp.sum(tot, axis=-1, keepdims=True) * inv_n
    sq = jnp.zeros((tt, lanes), F32)
    for c in range(n_cc):
        d = y_ref[c] - mu
        sq = sq + d * d
    rstd = lax.rsqrt(jnp.sum(sq, axis=-1, keepdims=True) * inv_n + EPS)
    for c in range(n_cc):
        un = (y_ref[c] - mu) * rstd * lng_ref[c] + lnb_ref[c]
        oc_ref[0, :, c * lanes:(c + 1) * lanes] = _silu(un).astype(BF16)


def _conv_finish(proj, o_f, o_b, onorm_g, dw_w, dw_b, ln_g, ln_b):
    bsz, _, l, lanes = proj.shape
    n_heads = o_f.shape[1]
    k_taps, bw = dw_w.shape
    n_cc = bw // lanes
    assert n_cc == n_heads and k_taps // 2 < CONV_HALO
    tt = _pick(l, 256, CONV_HALO)
    hb = tt // CONV_HALO
    n_halo = l // CONV_HALO
    seg_a = 5 * n_heads // n_cc
    tile = lambda seg: pl.BlockSpec((1, n_heads, tt, lanes), lambda b, i: (b, seg, i, 0))
    halo_prev = lambda seg: pl.BlockSpec((1, n_cc, CONV_HALO, lanes),
                                         lambda b, i: (b, seg, jnp.maximum(i * hb - 1, 0), 0))
    halo_next = lambda seg: pl.BlockSpec((1, n_cc, CONV_HALO, lanes),
                                         lambda b, i: (b, seg, jnp.minimum((i + 1) * hb, n_halo - 1), 0))
    full = lambda shape: pl.BlockSpec(shape, lambda b, i: (0,) * len(shape))
    per_chunk = lambda p: p.reshape(n_cc, 1, lanes)
    w = jnp.transpose(dw_w.reshape(k_taps, n_cc, lanes), (1, 0, 2))
    return pl.pallas_call(
        functools.partial(_conv_finish_body, n_heads=n_heads, n_cc=n_cc, k_taps=k_taps),
        grid=(bsz, l // tt),
        in_specs=[
            pl.BlockSpec((1, n_heads, tt, lanes), lambda b, i: (b, 0, i, 0)),
            pl.BlockSpec((1, n_heads, tt, lanes), lambda b, i: (b, 0, i, 0)),
            tile(4), tile(seg_a), tile(seg_a + 1),
            halo_prev(seg_a), halo_prev(seg_a + 1), halo_next(seg_a), halo_next(seg_a + 1),
            full((n_heads, 1, lanes)), full((n_cc, k_taps, lanes)),
            full((n_cc, 1, lanes)), full((n_cc, 1, lanes)), full((n_cc, 1, lanes)),
        ],
        out_specs=[
            pl.BlockSpec((1, tt, n_heads * lanes), lambda b, i: (b, i, 0)),
            pl.BlockSpec((1, tt, bw), lambda b, i: (b, i, 0)),
        ],
        out_shape=[
            jax.ShapeDtypeStruct((bsz, l, n_heads * lanes), BF16),
            jax.ShapeDtypeStruct((bsz, l, bw), BF16),
        ],
        scratch_shapes=[pltpu.VMEM((n_cc, tt + 2 * CONV_HALO, lanes), F32), pltpu.VMEM((n_cc, tt, lanes), F32)],
        compiler_params=_cparams(("parallel", "parallel"), 32),
        name="hgrn2_gate_conv_module",
    )(o_f, o_b, proj, proj, proj, proj, proj, proj, proj, onorm_g.reshape(n_heads, 1, lanes), w,
      per_chunk(dw_b), per_chunk(ln_g), per_chunk(ln_b))


def _rope_tables(n_tokens, head_dim, scale, rotate):
    lane = np.arange(V7X_LANES) % head_dim
    q4 = head_dim // 4
    col = (lane // (2 * q4)) * q4 + lane % q4
    sign = np.where(lane % (2 * q4) < q4, -1.0, 1.0).astype(np.float32)
    if not rotate:
        return (jnp.full((n_tokens, V7X_LANES), scale, F32), jnp.zeros((n_tokens, V7X_LANES), F32))
    t = jnp.arange(n_tokens)
    axis_dim = head_dim // 2
    inv_freq = ROPE_THETA ** (-jnp.arange(0, axis_dim, 2, dtype=F32) / axis_dim)
    ang = jnp.concatenate([(t // GRID_W).astype(F32)[:, None] * inv_freq,
                           (t % GRID_W).astype(F32)[:, None] * inv_freq], axis=-1)
    return (jnp.cos(ang)[:, col] * scale, jnp.sin(ang)[:, col] * (sign * scale))


def _qk_prep_one(x, gain, gmat, cos, sin, half_w):
    x = x.astype(F32)
    x2 = x * x
    hi = x2.astype(BF16)
    lo = (x2 - hi.astype(F32)).astype(BF16)
    y = x * lax.rsqrt(_dot(hi, gmat) + _dot(lo, gmat) + EPS) * gain
    lane = lax.broadcasted_iota(jnp.int32, y.shape, 1)
    partner = jnp.where(lane % (2 * half_w) < half_w,
                        pltpu.roll(y, V7X_LANES - half_w, 1), pltpu.roll(y, half_w, 1))
    return (y * cos + partner * sin).astype(BF16)


def _qk_prep_body(cq_ref, ck_ref, dq_ref, dk_ref, tab_ref, gain_ref, gmat_ref, o_ref, *, n_heads, n_kv):
    g64, g128 = gmat_ref[0], gmat_ref[1]
    d_half, g_half = HEAD_DIM // 8, HEAD_DIM // 4
    for h in range(n_heads):
        o_ref[0, h] = _qk_prep_one(cq_ref[0, h], gain_ref[0], g64, tab_ref[0], tab_ref[1], d_half)
        o_ref[0, n_heads + h] = _qk_prep_one(ck_ref[0, h], gain_ref[1], g64, tab_ref[2], tab_ref[3], d_half)
        o_ref[0, 2 * n_heads + h] = _qk_prep_one(dq_ref[0, h], gain_ref[2], g128, tab_ref[4], tab_ref[5], g_half)
    for h in range(n_kv):
        o_ref[0, 3 * n_heads + h] = _qk_prep_one(dk_ref[0, h], gain_ref[3], g128, tab_ref[6], tab_ref[7], g_half)


def _qk_prep(proj, gains, n_heads, n_kv, rotate):
    bsz, _, l, lanes = proj.shape
    diff_d = HEAD_DIM // 2
    log2e = math.log2(math.e)
    tabs = jnp.stack(_rope_tables(l, diff_d, diff_d ** -0.5 * log2e, rotate) + _rope_tables(l, diff_d, 1.0, rotate)
                     + _rope_tables(l, HEAD_DIM, HEAD_DIM ** -0.5 * log2e, rotate) + _rope_tables(l, HEAD_DIM, 1.0, rotate))
    lane = np.arange(lanes)
    gmat = np.stack([(lane[:, None] // diff_d == lane[None, :] // diff_d) / diff_d, np.full((lanes, lanes), 1.0 / lanes)])
    tt = _pick(l, 256, V7X_SUBLANES)
    tile = lambda n, idx: pl.BlockSpec((1, n, tt, lanes), lambda b, i: (b, idx, i, 0))
    n_out = 3 * n_heads + n_kv
    return pl.pallas_call(
        functools.partial(_qk_prep_body, n_heads=n_heads, n_kv=n_kv),
        grid=(bsz, l // tt),
        in_specs=[
            tile(n_heads, 0), tile(n_heads, 1), tile(n_heads, 3), tile(n_kv, 4 * n_heads // n_kv),
            pl.BlockSpec((8, tt, lanes), lambda b, i: (0, i, 0)),
            pl.BlockSpec((4, 1, lanes), lambda b, i: (0, 0, 0)),
            pl.BlockSpec((2, lanes, lanes), lambda b, i: (0, 0, 0)),
        ],
        out_specs=pl.BlockSpec((1, n_out, tt, lanes), lambda b, i: (b, 0, i, 0)),
        out_shape=jax.ShapeDtypeStruct((bsz, n_out, l, lanes), BF16),
        compiler_params=_cparams(("parallel", "parallel"), 32),
        name="qk_norm_rope",
    )(proj, proj, proj, proj, tabs, gains, jnp.asarray(gmat, BF16))


def _attn_body(lam_ref, subg_ref, q_ref, *refs, n_stack, diff, lam_init, chunks):
    n_src = len(chunks)
    kv_refs = refs[:2 * n_src]
    o_ref, qs_ref, m_ref, acc_ref = refs[2 * n_src:]
    tq = q_ref.shape[2]
    if diff:
        q = q_ref[0, 0]
        lane = lax.broadcasted_iota(jnp.int32, q.shape, 1)
        zero = jnp.zeros_like(q)
        qs_ref[0:tq] = jnp.where(lane < HEAD_DIM // 2, q, zero)
        qs_ref[tq:2 * tq] = jnp.where(lane >= HEAD_DIM // 2, q, zero)
    else:
        for g in range(n_stack):
            qs_ref[g * tq:(g + 1) * tq] = q_ref[0, g]
    m_ref[...] = jnp.full(m_ref.shape, NEG_BIG, F32)
    acc_ref[...] = jnp.zeros(acc_ref.shape, F32)
    lanes = V7X_LANES

    def visit(k, v):
        s = _dot_nt(qs_ref[...], k)
        n_t = s.shape[1] // lanes
        tiles = [s[:, t * lanes:(t + 1) * lanes] for t in range(n_t)]
        mt = tiles[0]
        for t in range(1, n_t):
            mt = jnp.maximum(mt, tiles[t])
        m_prev = m_ref[...]
        m_new = jnp.maximum(m_prev, jnp.max(mt, axis=-1, keepdims=True))
        alpha = jnp.exp2(m_prev - m_new)
        p = jnp.concatenate([jnp.exp2(t - m_new).astype(BF16) for t in tiles], axis=1)
        v_ext = jnp.concatenate([v, jnp.ones(v.shape, BF16)], axis=1)
        acc_ref[...] = jnp.concatenate([alpha, alpha], axis=1) * acc_ref[...] + _dot(p, v_ext)
        m_ref[...] = m_new

    for src, (tk, n_chunks) in enumerate(chunks):
        k_ref, v_ref = kv_refs[2 * src], kv_refs[2 * src + 1]
        if n_chunks == 1:
            visit(k_ref[0, 0], v_ref[0, 0])
        else:
            def step(ci, carry, k_ref=k_ref, v_ref=v_ref, tk=tk):
                rows = pl.ds(pl.multiple_of(ci * tk, tk), tk)
                visit(k_ref[0, 0, rows, :], v_ref[0, 0, rows, :])
                return carry
            lax.fori_loop(0, n_chunks, step, 0, unroll=8 if n_chunks % 8 == 0 else 1)

    o = acc_ref[:, 0:lanes] / acc_ref[:, lanes:2 * lanes]
    if diff:
        lp = lam_ref[...]
        lam = (jnp.exp(jnp.sum(lp[0:1] * lp[1:2], axis=-1, keepdims=True))
               - jnp.exp(jnp.sum(lp[2:3] * lp[3:4], axis=-1, keepdims=True)) + lam_init)
        d = o[0:tq] - lam * o[tq:2 * tq]
        y = d * lax.rsqrt(jnp.mean(d * d, axis=-1, keepdims=True) + EPS) * subg_ref[...]
        o_ref[0] = (y * (1.0 - lam_init)).astype(BF16)
    else:
        for g in range(n_stack):
            o_ref[0, :, g * HEAD_DIM:(g + 1) * HEAD_DIM] = o[g * tq:(g + 1) * tq].astype(BF16)


def _attention(qk_q, kv_srcs, lam_p, subln_g, lam_init, n_heads, n_kv, diff):
    bsz, _, lq, lanes = qk_q.shape
    group = n_heads // n_kv
    n_stack = 2 if diff else group
    n_units = n_heads if diff else n_kv
    tq = _pick(lq, 1024 // n_stack, V7X_SUBLANES)
    rows = n_stack * tq
    if diff:
        q_spec = pl.BlockSpec((1, 1, tq, lanes), lambda b, u, i: (b, u, i, 0))
        k_chunk = lambda u: n_heads + u
        v_chunk = lambda u: 2 * n_heads + u
        out_w = lanes
    else:
        q_spec = pl.BlockSpec((1, group, tq, lanes), lambda b, u, i: (b, 2 * n_heads // group + u, i, 0))
        k_chunk = lambda u: 3 * n_heads + u
        v_chunk = lambda u: 4 * n_heads + n_kv + u
        out_w = group * lanes
    in_specs = [pl.BlockSpec((4, lanes), lambda b, u, i: (0, 0)), pl.BlockSpec((1, lanes), lambda b, u, i: (0, 0)), q_spec]
    args = [lam_p, subln_g.reshape(1, lanes), qk_q]
    chunks = []
    for qk_k, proj in kv_srcs:
        lk = qk_k.shape[2]
        tk = _pick(lk, 512, V7X_SUBLANES)
        chunks.append((tk, lk // tk))
        in_specs.append(pl.BlockSpec((1, 1, lk, lanes), lambda b, u, i: (b, k_chunk(u), 0, 0)))
        in_specs.append(pl.BlockSpec((1, 1, lk, lanes), lambda b, u, i: (b, v_chunk(u), 0, 0)))
        args += [qk_k, proj]
    return pl.pallas_call(
        functools.partial(_attn_body, n_stack=n_stack, diff=diff, lam_init=lam_init, chunks=tuple(chunks)),
        grid=(bsz, n_units, lq // tq),
        in_specs=in_specs,
        out_specs=pl.BlockSpec((1, tq, out_w), lambda b, u, i: (b, i, u)),
        out_shape=jax.ShapeDtypeStruct((bsz, lq, n_heads * lanes), BF16),
        scratch_shapes=[pltpu.VMEM((rows, lanes), BF16), pltpu.VMEM((rows, lanes), F32),
                        pltpu.VMEM((rows, 2 * lanes), F32)],
        compiler_params=_cparams(("parallel", "parallel", "arbitrary"), 40),
        name="diff_attention" if diff else "gqa_attention",
    )(*args)


def kernel(x, c, ctx, c_ctx, ada_w, ada_b, norm_mix_g, norm_ffn_g, w_out, ab_w_in, hgrn_lb_logits, hgrn_onorm_g,
           conv_dw_w, conv_dw_b, conv_ln_g, conv_ln_b, cd_w_in, diff_qnorm_g, diff_knorm_g, diff_lambda, diff_subln_g,
           gqa_qnorm_g, gqa_knorm_g, ffn_w1, ffn_w3, ffn_w2, moe_router, moe_w1, moe_w3, moe_w2):
    bsz, _, d = x.shape
    depth = ada_w.shape[0]
    lanes = V7X_LANES
    n_heads = (d // 2) // HEAD_DIM
    n_kv = (cd_w_in.shape[2] - 4 * n_heads * HEAD_DIM) // (2 * HEAD_DIM)
    n_experts = moe_router.shape[2]

    rows = -(-(bsz + 1) // V7X_SUBLANES) * V7X_SUBLANES
    cond = jnp.zeros((rows, d), F32).at[:bsz].set(c).at[bsz].set(c_ctx)
    mods = _ada_all(cond, ada_w, ada_b)
    lbs = jnp.cumsum(jax.nn.softmax(hgrn_lb_logits.astype(F32), axis=0), axis=0)
    lbs = lbs - lbs[:1]

    for li in range(depth):
        j = li // 2
        need_ctx = li < depth - 1
        sh1, sc1, g1, sh2, sc2, g2 = [t.reshape(bsz, 1, d) for t in jnp.split(mods[li, :bsz], 6, axis=-1)]
        csh1, csc1, cg1, csh2, csc2, cg2 = [t.reshape(1, 1, d) for t in jnp.split(mods[li, bsz:bsz + 1], 6, axis=-1)]
        wo = _cast_bf16(w_out, li)
        if li % 2 == 0:
            w_in = _cast_bf16(ab_w_in, j)
            px = _norm_mm(x, norm_mix_g[li], sc1, sh1, w_in)
            pc = _norm_mm(ctx, norm_mix_g[li], csc1, csh1, w_in)
            lb_f = lbs[j, 0].reshape(n_heads, 1, HEAD_DIM)
            lb_b = lbs[j, 1].reshape(n_heads, 1, HEAD_DIM)
            s0 = jnp.zeros((bsz, n_heads, HEAD_DIM, HEAD_DIM), F32)
            ocf, s_f = _hgrn_scan(pc, lb_f, s0, n_heads, False)
            ocb, s_b = _hgrn_scan(pc, lb_b, s0, n_heads, True)
            oxf, _ = _hgrn_scan(px, lb_f, s_f, n_heads, False)
            oxb, _ = _hgrn_scan(px, lb_b, s_b, n_heads, True)
            conv_args = (hgrn_onorm_g[j], conv_dw_w[j], conv_dw_b[j], conv_ln_g[j], conv_ln_b[j])
            mix_x = _conv_finish(px, oxf, oxb, *conv_args)
            mix_c = _conv_finish(pc, ocf, ocb, *conv_args) if need_ctx else None
        else:
            w_in = _cast_bf16(cd_w_in, j)
            px = _norm_mm(x, norm_mix_g[li], sc1, sh1, w_in)
            pc = _norm_mm(ctx, norm_mix_g[li], csc1, csh1, w_in)
            tile2 = lambda g: jnp.tile(g, lanes // g.shape[0])
            gains = jnp.stack([tile2(diff_qnorm_g[j]), tile2(diff_knorm_g[j]), gqa_qnorm_g[j], gqa_knorm_g[j]])
            gains = gains.reshape(4, 1, lanes).astype(F32)
            qx = _qk_prep(px, gains, n_heads, n_kv, True)
            qc = _qk_prep(pc, gains, n_heads, n_kv, False)
            lam_init = 0.8 - 0.6 * math.exp(-0.3 * li)
            lam_p = jnp.zeros((4, lanes), F32).at[:, :diff_lambda.shape[2]].set(diff_lambda[j])
            attn = lambda q, srcs, diff: _attention(q, srcs, lam_p, diff_subln_g[j], lam_init, n_heads, n_kv, diff)
            srcs_x = [(qc, pc), (qx, px)]
            mix_x = (attn(qx, srcs_x, True), attn(qx, srcs_x, False))
            mix_c = (attn(qc, [(qc, pc)], True), attn(qc, [(qc, pc)], False)) if need_ctx else None
        x = _mm_resid(mix_x[0], mix_x[1], wo, x, g1)
        if need_ctx:
            ctx = _mm_resid(mix_c[0], mix_c[1], wo, ctx, cg1)

        if li % 2 == 0:
            mixer = functools.partial(_ffn, w1=_cast_bf16(ffn_w1, j), w3=_cast_bf16(ffn_w3, j), w2=_cast_bf16(ffn_w2, j))
        else:
            router = jnp.zeros((d, lanes), F32).at[:, :n_experts].set(moe_router[j])
            mixer = functools.partial(_moe, router=router, w1=_cast_bf16(moe_w1, j), w3=_cast_bf16(moe_w3, j),
                                      w2=_cast_bf16(moe_w2, j))
        x = mixer(x, norm_ffn_g[li], sc2, sh2, g2)
        if need_ctx:
            ctx = mixer(ctx, norm_ffn_g[li], csc2, csh2, cg2)
    return x
```
